```python
import jax, jax.numpy as jnp
from jax import lax
import numpy as np

D_MODEL = 1024
BATCH = 8
SEQ = 2048
DEPTH = 4

HEAD_DIM = 64
FOX_HEADS = 8
FOX_WIDTH = FOX_HEADS * HEAD_DIM
Q_BLOCK = 128
SWA_Q_HEADS = 8
SWA_KV_HEADS = 2
SWA_WIDTH = SWA_Q_HEADS * HEAD_DIM
SWA_KV_WIDTH = SWA_KV_HEADS * HEAD_DIM
WINDOW = 128
GLA_HEADS = 4
GLA_DK = D_MODEL // 2
GLA_DV = D_MODEL
GLA_HEAD_K = GLA_DK // GLA_HEADS
GLA_HEAD_V = GLA_DV // GLA_HEADS
GLA_RANK = 16
GLA_TAU = 16.0
GLA_CHUNK = 64
D_FF = 2816
N_BRANCH = 3
EPS = 1e-6
NEG_INF = -1e30
SPLIT_SIZES = (FOX_WIDTH, FOX_WIDTH, FOX_WIDTH, FOX_HEADS,
               SWA_WIDTH, SWA_KV_WIDTH, SWA_KV_WIDTH,
               GLA_DK, GLA_DK, GLA_DV, GLA_RANK, GLA_DV,
               N_BRANCH * D_MODEL)
D_IN = sum(SPLIT_SIZES)

kernel_name = "hybrid_fox_swa_gla_macaron"


def rmsnorm(x, w):
    xf = x.astype(jnp.float32)
    y = xf * lax.rsqrt(jnp.mean(xf * xf, axis=-1, keepdims=True) + EPS)
    return (y * w.astype(jnp.float32)).astype(x.dtype)


def swiglu(h, w_gu, w_down):
    g, u = jnp.split(h @ w_gu, 2, axis=-1)
    return (jax.nn.silu(g) * u) @ w_down


def forgetting_attention(q, k, v, f_logit, f_bias):
    B, S, H, Dh = q.shape
    nb = S // Q_BLOCK
    log_f = jax.nn.log_sigmoid(f_logit.astype(jnp.float32) + f_bias.astype(jnp.float32))
    c = jnp.cumsum(log_f, axis=1).transpose(0, 2, 1)
    qb = q.reshape(B, nb, Q_BLOCK, H, Dh).transpose(1, 0, 3, 2, 4)
    cq = c.reshape(B, H, nb, Q_BLOCK).transpose(2, 0, 1, 3)
    kpos = jnp.arange(S)
    scale = Dh ** -0.5

    def block(args):
        qi, ci, n = args
        s = jnp.einsum('bhqd,bkhd->bhqk', qi, k).astype(jnp.float32) * scale
        s = s + ci[..., None] - c[:, :, None, :]
        qpos = n * Q_BLOCK + jnp.arange(Q_BLOCK)
        s = jnp.where(kpos[None, :] <= qpos[:, None], s, NEG_INF)
        p = jax.nn.softmax(s, axis=-1)
        return jnp.einsum('bhqk,bkhd->bqhd', p.astype(v.dtype), v)

    o = lax.map(block, (qb, cq, jnp.arange(nb)))
    return o.transpose(1, 0, 2, 3, 4).reshape(B, S, H * Dh)


def sliding_window_attention(q, k, v, sinks):
    B, S, Hq, Dh = q.shape
    Hkv = k.shape[2]
    G = Hq // Hkv
    nb = S // WINDOW
    qb = q.reshape(B, nb, WINDOW, Hkv, G, Dh)
    kb = k.reshape(B, nb, WINDOW, Hkv, Dh)
    vb = v.reshape(B, nb, WINDOW, Hkv, Dh)
    prev = lambda t: jnp.concatenate([jnp.zeros_like(t[:, :1]), t[:, :-1]], axis=1)
    kw = jnp.concatenate([prev(kb), kb], axis=2)
    vw = jnp.concatenate([prev(vb), vb], axis=2)
    s = jnp.einsum('bnqhgd,bnkhd->bnhgqk', qb, kw).astype(jnp.float32) * (Dh ** -0.5)
    i = jnp.arange(WINDOW)[:, None]
    j = jnp.arange(2 * WINDOW)[None, :]
    band = (j > i) & (j <= i + WINDOW)
    has_prev = jnp.arange(nb)[:, None, None] > 0
    valid = band[None] & (has_prev | (j >= WINDOW)[None])
    s = jnp.where(valid[None, :, None, None], s, NEG_INF)
    sink = sinks.astype(jnp.float32).reshape(Hkv, G)[None, None, :, :, None, None]
    sink = jnp.broadcast_to(sink, s.shape[:-1] + (1,))
    p = jax.nn.softmax(jnp.concatenate([s, sink], axis=-1), axis=-1)[..., :-1]
    o = jnp.einsum('bnhgqk,bnkhd->bnqhgd', p.astype(v.dtype), vw)
    return o.reshape(B, S, Hq * Dh)


def gated_linear_attention(q, k, v, log_a):
    B, S, H, Dk = q.shape
    Dv = v.shape[-1]
    C = GLA_CHUNK
    N = S // C
    qf = q.astype(jnp.float32).reshape(B, N, C, H, Dk) * (Dk ** -0.5)
    kf = k.astype(jnp.float32).reshape(B, N, C, H, Dk)
    vf = v.astype(jnp.float32).reshape(B, N, C, H, Dv)
    b = jnp.cumsum(log_a.reshape(B, N, C, H, Dk), axis=2)
    b_last = b[:, :, -1:]
    q_t = qf * jnp.exp(b)
    k_t = kf * jnp.exp(-b)
    k_s = kf * jnp.exp(b_last - b)
    causal = jnp.tril(jnp.ones((C, C), dtype=bool))
    A = jnp.where(causal, jnp.einsum('bnihd,bnjhd->bnhij', q_t, k_t), 0.0)
    o_intra = jnp.einsum('bnhij,bnjhe->bnihe', A, vf)
    kv = jnp.einsum('bnjhd,bnjhe->bnhde', k_s, vf)
    decay = jnp.exp(b_last[:, :, 0])

    def step(state, inp):
        d, kv_n = inp
        return d[..., None] * state + kv_n, state

    state0 = jnp.zeros((B, H, Dk, Dv), jnp.float32)
    _, s_before = lax.scan(step, state0, (decay.transpose(1, 0, 2, 3), kv.transpose(1, 0, 2, 3, 4)))
    s_before = s_before.transpose(1, 0, 2, 3, 4)
    o_inter = jnp.einsum('bnihd,bnhde->bnihe', q_t, s_before)
    return (o_intra + o_inter).reshape(B, S, H, Dv)


def hybrid_mixer(h, w_in, fox_forget_bias, swa_sinks, gla_gate_w2, gla_gate_bias, gla_norm,
                 w_branch_fox, w_branch_swa, w_branch_gla, w_out):
    B, S, D = h.shape
    points = [int(p) for p in np.cumsum(SPLIT_SIZES)[:-1]]
    (fq, fk, fv, ff, sq, sk, sv, gq, gk, gv, glr, gr, gates) = jnp.split(h @ w_in, points, axis=-1)
    hd = lambda t, n: t.reshape(B, S, n, -1)
    o_fox = forgetting_attention(hd(fq, FOX_HEADS), hd(fk, FOX_HEADS), hd(fv, FOX_HEADS), ff, fox_forget_bias)
    o_swa = sliding_window_attention(hd(sq, SWA_Q_HEADS), hd(sk, SWA_KV_HEADS), hd(sv, SWA_KV_HEADS), swa_sinks)
    gate_logit = (glr @ gla_gate_w2 + gla_gate_bias).astype(jnp.float32)
    log_a = jax.nn.log_sigmoid(gate_logit) / GLA_TAU
    o = gated_linear_attention(hd(gq, GLA_HEADS), hd(gk, GLA_HEADS), hd(gv, GLA_HEADS), hd(log_a, GLA_HEADS))
    o = o * lax.rsqrt(jnp.mean(o * o, axis=-1, keepdims=True) + EPS) * gla_norm.astype(jnp.float32)
    o_gla = (o.reshape(B, S, GLA_DV) * jax.nn.silu(gr.astype(jnp.float32))).astype(h.dtype)
    g = jax.nn.sigmoid(gates.astype(jnp.float32)).astype(h.dtype).reshape(B, S, N_BRANCH, D)
    merged = (g[:, :, 0] * (o_fox @ w_branch_fox)
              + g[:, :, 1] * (o_swa @ w_branch_swa)
              + g[:, :, 2] * (o_gla @ w_branch_gla))
    return merged @ w_out


def setup_inputs(seed: int = 0) -> dict:
    key = jax.random.key(seed)
    ks = jax.random.split(key, 24)
    L, D, F = DEPTH, D_MODEL, D_FF
    f32 = jnp.float32
    nrm = lambda k, shape, fan_in: jax.random.normal(k, shape, f32) * (fan_in ** -0.5)
    gain = lambda k, shape: 1.0 + 0.05 * jax.random.normal(k, shape, f32)
    return {
        "x": jax.random.normal(ks[0], (BATCH, SEQ, D), f32),
        "ffn1_norm": gain(ks[1], (L, D)),
        "ffn1_w_gu": nrm(ks[2], (L, D, 2 * F), D),
        "ffn1_w_down": nrm(ks[3], (L, F, D), F),
        "mix_norm": gain(ks[4], (L, D)),
        "w_in": nrm(ks[5], (L, D, D_IN), D),
        "fox_forget_bias": jax.random.uniform(ks[6], (L, FOX_HEADS), f32, 1.0, 4.0),
        "swa_sinks": 0.5 * jax.random.normal(ks[7], (L, SWA_Q_HEADS), f32),
        "gla_gate_w2": nrm(ks[8], (L, GLA_RANK, GLA_DK), GLA_RANK),
        "gla_gate_bias": 0.1 * jax.random.normal(ks[9], (L, GLA_DK), f32),
        "gla_norm": gain(ks[10], (L, GLA_HEAD_V)),
        "w_branch_fox": nrm(ks[11], (L, FOX_WIDTH, D), FOX_WIDTH),
        "w_branch_swa": nrm(ks[12], (L, SWA_WIDTH, D), SWA_WIDTH),
        "w_branch_gla": nrm(ks[13], (L, GLA_DV, D), GLA_DV),
        "w_out": nrm(ks[14], (L, D, D), D),
        "ffn2_norm": gain(ks[15], (L, D)),
        "ffn2_w_gu": nrm(ks[16], (L, D, 2 * F), D),
        "ffn2_w_down": nrm(ks[17], (L, F, D), F),
        "final_norm": gain(ks[18], (D,)),
    }


def reference(x, ffn1_norm, ffn1_w_gu, ffn1_w_down, mix_norm, w_in, fox_forget_bias, swa_sinks,
              gla_gate_w2, gla_gate_bias, gla_norm, w_branch_fox, w_branch_swa, w_branch_gla, w_out,
              ffn2_norm, ffn2_w_gu, ffn2_w_down, final_norm):
    for l in range(DEPTH):
        x = x + 0.5 * swiglu(rmsnorm(x, ffn1_norm[l]), ffn1_w_gu[l], ffn1_w_down[l])
        x = x + hybrid_mixer(rmsnorm(x, mix_norm[l]), w_in[l], fox_forget_bias[l], swa_sinks[l],
                             gla_gate_w2[l], gla_gate_bias[l], gla_norm[l],
                             w_branch_fox[l], w_branch_swa[l], w_branch_gla[l], w_out[l])
        x = x + 0.5 * swiglu(rmsnorm(x, ffn2_norm[l]), ffn2_w_gu[l], ffn2_w_down[l])
    return rmsnorm(x, final_norm)
```

```python
import functools

import jax
import jax.numpy as jnp
from jax import lax
from jax.experimental import pallas as pl
from jax.experimental.pallas import tpu as pltpu

F32 = jnp.float32
BF16 = jnp.bfloat16

EPS = 1e-6
NEG_INF = -1e30
HEAD_DIM = 64
FOX_HEADS = 8
SWA_Q_HEADS = 8
SWA_KV_HEADS = 2
WINDOW = 128
GLA_HEADS = 4
GLA_HEAD_K = 128
GLA_HEAD_V = 256
GLA_RANK = 16
GLA_TAU = 16.0
GLA_CHUNK = 64
GLA_SUPER = 256
FOX_Q_TILE = 512
LANES = 128
VMEM_LIMIT_BYTES = 56 * 1024 * 1024

RAW_FQ, RAW_FK, RAW_FV = 0, 512, 1024
RAW_SQ, RAW_SK, RAW_SV = 1536, 2048, 2304
RAW_GQ, RAW_GK, RAW_GV = 2560, 3072, 3584
RAW_WIDTH = 4608
ACT_WIDTH = 4096
SMALL_WIDTH = 128


def _params(n_grid_axes):
    return pltpu.CompilerParams(
        dimension_semantics=("arbitrary",) * n_grid_axes,
        vmem_limit_bytes=VMEM_LIMIT_BYTES)


def _resident(shape):
    nd = len(shape)
    return pl.BlockSpec(shape, lambda *_: (0,) * nd, pipeline_mode=pl.Buffered(1))


def _rmsnorm(x, w):
    return x * lax.rsqrt(jnp.mean(x * x, axis=-1, keepdims=True) + EPS) * w


def _log_sigmoid(x):
    return jnp.minimum(x, 0.0) - jnp.log1p(jnp.exp(-jnp.abs(x)))


def _split3(x):
    p1 = x.astype(BF16)
    r1 = x - p1.astype(F32)
    p2 = r1.astype(BF16)
    p3 = (r1 - p2.astype(F32)).astype(BF16)
    return p1, p2, p3


def _dot(a, b):
    return jnp.dot(a, b, preferred_element_type=F32)


def _dot_nt(a, b):
    return lax.dot_general(a, b, (((1,), (1,)), ((), ())), preferred_element_type=F32)


def _dot_tn(a, b):
    return lax.dot_general(a, b, (((0,), (0,)), ((), ())), preferred_element_type=F32)


def _exact_tri_cumsum(tri, x):
    p1, p2, p3 = _split3(x)
    return _dot(tri, p1) + _dot(tri, p2) + _dot(tri, p3)


def _ffn_kernel(x_ref, nw_ref, wg_ref, wu_ref, wd_ref, fw_ref, o_ref, h_ref, a_ref, *,
                f_chunk, final_norm):
    h_ref[...] = _rmsnorm(x_ref[...], nw_ref[...]).astype(BF16)
    d_ff = wg_ref.shape[1]
    for c in range(d_ff // f_chunk):
        sl = slice(c * f_chunk, (c + 1) * f_chunk)
        g = _dot(h_ref[...], wg_ref[:, sl])
        u = _dot(h_ref[...], wu_ref[:, sl])
        a_ref[:, sl] = (g * jax.nn.sigmoid(g) * u).astype(BF16)
    y = x_ref[...] + 0.5 * _dot(a_ref[...], wd_ref[...])
    if final_norm:
        y = _rmsnorm(y, fw_ref[...])
    o_ref[...] = y


def _ffn(x, nw, wg, wu, wd, fw, *, final_norm, tm=512, f_chunk=256):
    t, d = x.shape
    d_ff = wg.shape[1]
    row = pl.BlockSpec((tm, d), lambda i: (i, 0))
    return pl.pallas_call(
        functools.partial(_ffn_kernel, f_chunk=f_chunk, final_norm=final_norm),
        grid=(t // tm,),
        in_specs=[row, _resident((1, d)), _resident((d, d_ff)), _resident((d, d_ff)),
                  _resident((d_ff, d)), _resident((1, d))],
        out_specs=row,
        out_shape=jax.ShapeDtypeStruct((t, d), F32),
        scratch_shapes=[pltpu.VMEM((tm, d), BF16), pltpu.VMEM((tm, d_ff), BF16)],
        compiler_params=_params(1),
        name="ffn",
    )(x, nw, wg, wu, wd, fw)


def _inproj_kernel(x_ref, nw_ref, wraw_ref, wact_ref, wsm_ref, w2_ref, fb_ref, gb_ref,
                   raw_ref, act_ref, logf_ref, bcum_ref, h_ref, *, n_chunk, gr_width):
    tm = x_ref.shape[0]
    h_ref[...] = _rmsnorm(x_ref[...], nw_ref[...]).astype(BF16)
    for c in range(raw_ref.shape[1] // n_chunk):
        sl = slice(c * n_chunk, (c + 1) * n_chunk)
        raw_ref[:, sl] = _dot(h_ref[...], wraw_ref[:, sl]).astype(BF16)
    for c in range(act_ref.shape[1] // n_chunk):
        sl = slice(c * n_chunk, (c + 1) * n_chunk)
        z = _dot(h_ref[...], wact_ref[:, sl])
        if c * n_chunk < gr_width:
            act_ref[:, sl] = (z * jax.nn.sigmoid(z)).astype(BF16)
        else:
            act_ref[:, sl] = jax.nn.sigmoid(z).astype(BF16)
    small = _dot(h_ref[...], wsm_ref[...])
    logf_ref[...] = _log_sigmoid(small + fb_ref[...])
    gate_logit = _dot(small.astype(BF16), w2_ref[...]) + gb_ref[...]
    log_a = _log_sigmoid(gate_logit) * (1.0 / GLA_TAU)
    r = lax.broadcasted_iota(jnp.int32, (GLA_SUPER, GLA_SUPER), 0)
    c = lax.broadcasted_iota(jnp.int32, (GLA_SUPER, GLA_SUPER), 1)
    tri = jnp.where((r >= c) & ((r // GLA_CHUNK) == (c // GLA_CHUNK)), 1.0, 0.0).astype(BF16)
    for s in range(tm // GLA_SUPER):
        rows = slice(s * GLA_SUPER, (s + 1) * GLA_SUPER)
        bcum_ref[rows, :] = _exact_tri_cumsum(tri, log_a[rows, :])


def _inproj(x, nw, w_raw, w_act, w_small, w2p, fbias, gbias, *, tm=512, n_chunk=512):
    t, d = x.shape
    gk = w2p.shape[1]
    row = lambda w: pl.BlockSpec((tm, w), lambda i: (i, 0))
    return pl.pallas_call(
        functools.partial(_inproj_kernel, n_chunk=n_chunk, gr_width=GLA_HEADS * GLA_HEAD_V),
        grid=(t // tm,),
        in_specs=[row(d), _resident((1, d)), _resident((d, RAW_WIDTH)), _resident((d, ACT_WIDTH)),
                  _resident((d, SMALL_WIDTH)), _resident((SMALL_WIDTH, gk)),
                  _resident((1, SMALL_WIDTH)), _resident((1, gk))],
        out_specs=[row(RAW_WIDTH), row(ACT_WIDTH), row(SMALL_WIDTH), row(gk)],
        out_shape=[jax.ShapeDtypeStruct((t, RAW_WIDTH), BF16),
                   jax.ShapeDtypeStruct((t, ACT_WIDTH), BF16),
                   jax.ShapeDtypeStruct((t, SMALL_WIDTH), F32),
                   jax.ShapeDtypeStruct((t, gk), F32)],
        scratch_shapes=[pltpu.VMEM((tm, d), BF16)],
        compiler_params=_params(1),
        name="inproj",
    )(x, nw, w_raw, w_act, w_small, w2p, fbias, gbias)


def _fox_kernel(q_ref, k_ref, v_ref, logf_ref, o_ref,
                c_ref, qa_ref, qb_ref, ka_ref, kb_ref, m_ref, l_ref, acc_ref, *, tq):
    s_len = q_ref.shape[0]
    pair = pl.program_id(1)
    blk = LANES

    r = lax.broadcasted_iota(jnp.int32, (blk, blk), 0)
    cc = lax.broadcasted_iota(jnp.int32, (blk, blk), 1)
    tri = jnp.where(r >= cc, 1.0, 0.0).astype(BF16)
    carry = jnp.zeros((1, LANES), F32)
    for n in range(s_len // blk):
        rows = slice(n * blk, (n + 1) * blk)
        within = _exact_tri_cumsum(tri, logf_ref[rows, :]) + carry
        c_ref[rows, :] = within
        carry = within[blk - 1:blk, :]

    lane = lax.broadcasted_iota(jnp.int32, (s_len, LANES), 1)
    low = lane < HEAD_DIM
    qf = q_ref[...].astype(F32) * (HEAD_DIM ** -0.5)
    kf = k_ref[...].astype(F32)
    c_all = c_ref[...]
    for head, (q_out, k_out) in enumerate(((qa_ref, ka_ref), (qb_ref, kb_ref))):
        own = low if head == 0 else jnp.logical_not(low)
        aug = lane - HEAD_DIM if head == 0 else lane
        ch = jnp.sum(jnp.where(lane == 2 * pair + head, c_all, 0.0), axis=-1, keepdims=True)
        c1, c2, c3 = (p.astype(F32) for p in _split3(ch))
        q_aug = jnp.where(aug == 0, c1, jnp.where(aug == 1, c2, jnp.where(aug == 2, c3,
                jnp.where((aug >= 3) & (aug < 6), 1.0, 0.0))))
        k_aug = jnp.where((aug >= 0) & (aug < 3), 1.0, jnp.where(aug == 3, -c1,
                jnp.where(aug == 4, -c2, jnp.where(aug == 5, -c3, 0.0))))
        q_out[...] = jnp.where(own, qf, q_aug).astype(BF16)
        k_out[...] = jnp.where(own, kf, k_aug).astype(BF16)

    def attend(h, q, k, v, mask):
        s = _dot_nt(q, k)
        if mask is not None:
            s = jnp.where(mask, s, NEG_INF)
        m_prev = m_ref[h]
        m_new = jnp.maximum(m_prev, jnp.max(s, axis=-1, keepdims=True))
        alpha = jnp.exp(m_prev - m_new)
        p = jnp.exp(s - m_new)
        l_ref[h] = alpha * l_ref[h] + jnp.sum(p, axis=-1, keepdims=True)
        acc_ref[h] = alpha * acc_ref[h] + _dot(p.astype(BF16), v)
        m_ref[h] = m_new

    ri = lax.broadcasted_iota(jnp.int32, (tq, tq), 0)
    ci = lax.broadcasted_iota(jnp.int32, (tq, tq), 1)
    causal = ci <= ri
    out_low = lax.broadcasted_iota(jnp.int32, (tq, LANES), 1) < HEAD_DIM

    for qi in range(s_len // tq):
        q_rows = slice(qi * tq, (qi + 1) * tq)
        m_ref[...] = jnp.full(m_ref.shape, NEG_INF, F32)
        l_ref[...] = jnp.zeros(l_ref.shape, F32)
        acc_ref[...] = jnp.zeros(acc_ref.shape, F32)

        def kv_step(kj, _):
            k_rows = pl.ds(pl.multiple_of(kj * tq, tq), tq)
            v = v_ref[k_rows, :]
            attend(0, qa_ref[q_rows, :], ka_ref[k_rows, :], v, None)
            attend(1, qb_ref[q_rows, :], kb_ref[k_rows, :], v, None)
            return 0

        if qi > 0:
            lax.fori_loop(0, qi, kv_step, 0)
        v = v_ref[q_rows, :]
        attend(0, qa_ref[q_rows, :], ka_ref[q_rows, :], v, causal)
        attend(1, qb_ref[q_rows, :], kb_ref[q_rows, :], v, causal)
        o_ref[q_rows, :] = jnp.where(out_low, acc_ref[0] / l_ref[0],
                                     acc_ref[1] / l_ref[1]).astype(BF16)


def _fox(raw, logf, *, batch, seq):
    t = raw.shape[0]
    tq = min(FOX_Q_TILE, seq)
    n_pairs = FOX_HEADS // 2
    blk = lambda off: pl.BlockSpec((seq, LANES), lambda b, j: (b, off // LANES + j))
    return pl.pallas_call(
        functools.partial(_fox_kernel, tq=tq),
        grid=(batch, n_pairs),
        in_specs=[blk(RAW_FQ), blk(RAW_FK), blk(RAW_FV),
                  pl.BlockSpec((seq, LANES), lambda b, j: (b, 0))],
        out_specs=pl.BlockSpec((seq, LANES), lambda b, j: (b, j)),
        out_shape=jax.ShapeDtypeStruct((t, FOX_HEADS * HEAD_DIM), BF16),
        scratch_shapes=[pltpu.VMEM((seq, LANES), F32)]
                       + [pltpu.VMEM((seq, LANES), BF16)] * 4
                       + [pltpu.VMEM((2, tq, 1), F32), pltpu.VMEM((2, tq, 1), F32),
                          pltpu.VMEM((2, tq, LANES), F32)],
        compiler_params=_params(2),
        name="fox",
    )(raw, raw, raw, logf)


def _swa_kernel(sink_ref, q_ref, k_ref, v_ref, o_ref):
    s_len = q_ref.shape[0]
    w = WINDOW
    group = SWA_Q_HEADS // SWA_KV_HEADS
    scale = HEAD_DIM ** -0.5
    low = lax.broadcasted_iota(jnp.int32, (w, LANES), 1) < HEAD_DIM

    def block(n, kvh, first):
        q_rows = pl.ds(pl.multiple_of(n * w, w), w)
        qs = []
        for p in range(group // 2):
            qp = q_ref[q_rows, (kvh * group // 2 + p) * LANES:(kvh * group // 2 + p + 1) * LANES]
            zero = jnp.zeros_like(qp)
            qs += [jnp.where(low, qp, zero), jnp.where(low, zero, qp)]
        q = jnp.concatenate(qs, axis=0)
        kw_len = w if first else 2 * w
        k_rows = q_rows if first else pl.ds(pl.multiple_of((n - 1) * w, w), 2 * w)
        kv_cols = slice(kvh * LANES, (kvh + 1) * LANES)
        kk = k_ref[k_rows, kv_cols]
        vv = v_ref[k_rows, kv_cols]
        s = _dot_nt(q, kk) * scale
        i = lax.broadcasted_iota(jnp.int32, (group * w, kw_len), 0) % w
        j = lax.broadcasted_iota(jnp.int32, (group * w, kw_len), 1)
        valid = (j <= i) if first else ((j > i) & (j <= i + w))
        s = jnp.where(valid, s, NEG_INF)
        hrow = lax.broadcasted_iota(jnp.int32, (group * w, 1), 0) // w
        sink = jnp.zeros((group * w, 1), F32)
        for g in range(group):
            sink = jnp.where(hrow == g, sink_ref[kvh * group + g], sink)
        m = jnp.maximum(jnp.max(s, axis=-1, keepdims=True), sink)
        p = jnp.exp(s - m)
        denom = jnp.sum(p, axis=-1, keepdims=True) + jnp.exp(sink - m)
        o = _dot(p.astype(BF16), vv) / denom
        for pidx in range(group // 2):
            oa = o[(2 * pidx) * w:(2 * pidx + 1) * w, :]
            ob = o[(2 * pidx + 1) * w:(2 * pidx + 2) * w, :]
            col = (kvh * group // 2 + pidx) * LANES
            o_ref[q_rows, col:col + LANES] = jnp.where(low, oa, ob).astype(BF16)

    for kvh in range(SWA_KV_HEADS):
        block(0, kvh, True)

        def body(n, _, kvh=kvh):
            block(n, kvh, False)
            return 0

        lax.fori_loop(1, s_len // w, body, 0)


def _swa(raw, sinks, *, batch, seq):
    t = raw.shape[0]
    qw = SWA_Q_HEADS * HEAD_DIM
    kw = SWA_KV_HEADS * LANES
    return pl.pallas_call(
        _swa_kernel,
        grid=(batch,),
        in_specs=[pl.BlockSpec(memory_space=pltpu.SMEM),
                  pl.BlockSpec((seq, qw), lambda b: (b, RAW_SQ // qw)),
                  pl.BlockSpec((seq, kw), lambda b: (b, RAW_SK // kw)),
                  pl.BlockSpec((seq, kw), lambda b: (b, RAW_SV // kw))],
        out_specs=pl.BlockSpec((seq, qw), lambda b: (b, 0)),
        out_shape=jax.ShapeDtypeStruct((t, qw), BF16),
        compiler_params=_params(1),
        name="swa",
    )(sinks, raw, raw, raw)


def _gla_kernel(q_ref, k_ref, v_ref, b_ref, gr_ref, nw_ref, o_ref,
                qt_ref, oi_ref, kv_ref, dec_ref, st_ref):
    s_len = q_ref.shape[0]
    ck, sup = GLA_CHUNK, GLA_SUPER
    per = sup // ck
    scale = GLA_HEAD_K ** -0.5
    r = lax.broadcasted_iota(jnp.int32, (sup, sup), 0)
    c = lax.broadcasted_iota(jnp.int32, (sup, sup), 1)
    intra = (r >= c) & ((r // ck) == (c // ck))

    def pass1(sb, _):
        rows = pl.ds(pl.multiple_of(sb * sup, sup), sup)
        b = b_ref[rows, :]
        b3 = b.reshape(per, ck, GLA_HEAD_K)
        b_last = b3[:, ck - 1:ck, :]
        qf = q_ref[rows, :].astype(F32) * scale
        kf = k_ref[rows, :].astype(F32)
        q_t = (qf * jnp.exp(b)).astype(BF16)
        k_t = (kf * jnp.exp(-b)).astype(BF16)
        k_s = (kf.reshape(per, ck, GLA_HEAD_K) * jnp.exp(b_last - b3)).astype(BF16)
        v = v_ref[rows, :]
        a = jnp.where(intra, _dot_nt(q_t, k_t), 0.0)
        qt_ref[rows, :] = q_t
        oi_ref[rows, :] = _dot(a.astype(BF16), v)
        for cidx in range(per):
            n = sb * per + cidx
            vc = v[cidx * ck:(cidx + 1) * ck, :]
            kv_ref[n] = _dot_tn(vc, k_s[cidx])
            dec_ref[n] = jnp.exp(b_last[cidx])
        return 0

    lax.fori_loop(0, s_len // sup, pass1, 0)

    st_ref[...] = jnp.zeros(st_ref.shape, F32)

    def pass2(n, _):
        rows = pl.ds(pl.multiple_of(n * ck, ck), ck)
        st = st_ref[...]
        o = oi_ref[rows, :] + _dot_nt(qt_ref[rows, :], st.astype(BF16))
        st_ref[...] = st * dec_ref[n] + kv_ref[n]
        o = o * lax.rsqrt(jnp.mean(o * o, axis=-1, keepdims=True) + EPS) * nw_ref[...]
        o_ref[rows, :] = (o * gr_ref[rows, :].astype(F32)).astype(BF16)
        return 0

    lax.fori_loop(0, s_len // ck, pass2, 0)


def _gla(raw, bcum, act, gla_norm, *, batch, seq):
    t = raw.shape[0]
    n_chunks = seq // GLA_CHUNK
    dk, dv = GLA_HEAD_K, GLA_HEAD_V
    return pl.pallas_call(
        _gla_kernel,
        grid=(batch, GLA_HEADS),
        in_specs=[pl.BlockSpec((seq, dk), lambda b, h: (b, RAW_GQ // dk + h)),
                  pl.BlockSpec((seq, dk), lambda b, h: (b, RAW_GK // dk + h)),
                  pl.BlockSpec((seq, dv), lambda b, h: (b, RAW_GV // dv + h)),
                  pl.BlockSpec((seq, dk), lambda b, h: (b, h)),
                  pl.BlockSpec((seq, dv), lambda b, h: (b, h)),
                  pl.BlockSpec((1, dv), lambda b, h: (0, 0))],
        out_specs=pl.BlockSpec((seq, dv), lambda b, h: (b, h)),
        out_shape=jax.ShapeDtypeStruct((t, GLA_HEADS * dv), BF16),
        scratch_shapes=[pltpu.VMEM((seq, dk), BF16), pltpu.VMEM((seq, dv), F32),
                        pltpu.VMEM((n_chunks, dv, dk), F32), pltpu.VMEM((n_chunks, 1, dk), F32),
                        pltpu.VMEM((dv, dk), F32)],
        compiler_params=_params(2),
        name="gla",
    )(raw, raw, raw, bcum, act, gla_norm)


def _merge_kernel(x_ref, of_ref, os_ref, og_ref, g0_ref, g1_ref, g2_ref,
                  wf_ref, ws_ref, wg_ref, wo_ref, o_ref):
    merged = (g0_ref[...].astype(F32) * _dot(of_ref[...], wf_ref[...])
              + g1_ref[...].astype(F32) * _dot(os_ref[...], ws_ref[...])
              + g2_ref[...].astype(F32) * _dot(og_ref[...], wg_ref[...]))
    o_ref[...] = x_ref[...] + _dot(merged.astype(BF16), wo_ref[...])


def _merge(x, o_fox, o_swa, o_gla, act, wf, ws, wg, wo, *, tm=512):
    t, d = x.shape
    row = lambda w, j=0: pl.BlockSpec((tm, w), lambda i: (i, j))
    return pl.pallas_call(
        _merge_kernel,
        grid=(t // tm,),
        in_specs=[row(d), row(o_fox.shape[1]), row(o_swa.shape[1]), row(o_gla.shape[1]),
                  row(d, 1), row(d, 2), row(d, 3),
                  _resident(wf.shape), _resident(ws.shape), _resident(wg.shape),
                  _resident(wo.shape)],
        out_specs=row(d),
        out_shape=jax.ShapeDtypeStruct((t, d), F32),
        compiler_params=_params(1),
        name="merge",
    )(x, o_fox, o_swa, o_gla, act, act, act, wf, ws, wg, wo)


def _prep_w_in(w_in):
    sizes = (512, 512, 512, FOX_HEADS, 512, 128, 128, 512, 512, 1024, GLA_RANK, 1024, 3072)
    offs = [0]
    for s in sizes:
        offs.append(offs[-1] + s)
    (fq, fk, fv, ff, sq, sk, sv, gq, gk, gv, glr, gr, gates) = (
        w_in[..., offs[i]:offs[i + 1]] for i in range(len(sizes)))

    def dup_heads(w):
        return jnp.concatenate([w[..., :HEAD_DIM], w[..., :HEAD_DIM],
                                w[..., HEAD_DIM:], w[..., HEAD_DIM:]], axis=-1)

    w_raw = jnp.concatenate([fq, fk, fv, sq, dup_heads(sk), dup_heads(sv), gq, gk, gv], axis=-1)
    w_act = jnp.concatenate([gr, gates], axis=-1)
    pad = jnp.zeros(w_in.shape[:-1] + (SMALL_WIDTH - FOX_HEADS - GLA_RANK,), w_in.dtype)
    w_small = jnp.concatenate([ff, glr, pad], axis=-1)
    return w_raw.astype(BF16), w_act.astype(BF16), w_small.astype(BF16)


def kernel(x, ffn1_norm, ffn1_w_gu, ffn1_w_down, mix_norm, w_in, fox_forget_bias, swa_sinks,
           gla_gate_w2, gla_gate_bias, gla_norm, w_branch_fox, w_branch_swa, w_branch_gla, w_out,
           ffn2_norm, ffn2_w_gu, ffn2_w_down, final_norm):
    batch, seq, d = x.shape
    depth = w_in.shape[0]
    d_ff = ffn1_w_down.shape[1]
    xt = x.reshape(batch * seq, d)

    w_raw, w_act, w_small = _prep_w_in(w_in)
    w2p = jnp.zeros((depth, SMALL_WIDTH, gla_gate_w2.shape[-1]), BF16)
    w2p = w2p.at[:, FOX_HEADS:FOX_HEADS + GLA_RANK, :].set(gla_gate_w2.astype(BF16))
    fbias = jnp.zeros((depth, 1, SMALL_WIDTH), F32).at[:, 0, :FOX_HEADS].set(fox_forget_bias)
    f1g, f1u = ffn1_w_gu[..., :d_ff].astype(BF16), ffn1_w_gu[..., d_ff:].astype(BF16)
    f2g, f2u = ffn2_w_gu[..., :d_ff].astype(BF16), ffn2_w_gu[..., d_ff:].astype(BF16)
    f1d, f2d = ffn1_w_down.astype(BF16), ffn2_w_down.astype(BF16)
    wbf, wbs = w_branch_fox.astype(BF16), w_branch_swa.astype(BF16)
    wbg, wo = w_branch_gla.astype(BF16), w_out.astype(BF16)
    fw = final_norm.reshape(1, d)

    for l in range(depth):
        xt = _ffn(xt, ffn1_norm[l].reshape(1, d), f1g[l], f1u[l], f1d[l], fw, final_norm=False)
        raw, act, logf, bcum = _inproj(xt, mix_norm[l].reshape(1, d), w_raw[l], w_act[l],
                                       w_small[l], w2p[l], fbias[l],
                                       gla_gate_bias[l].reshape(1, -1))
        o_fox = _fox(raw, logf, batch=batch, seq=seq)
        o_swa = _swa(raw, swa_sinks[l], batch=batch, seq=seq)
        o_gla = _gla(raw, bcum, act, gla_norm[l].reshape(1, -1), batch=batch, seq=seq)
        xt = _merge(xt, o_fox, o_swa, o_gla, act, wbf[l], wbs[l], wbg[l], wo[l])
        xt = _ffn(xt, ffn2_norm[l].reshape(1, d), f2g[l], f2u[l], f2d[l], fw,
                  final_norm=(l == depth - 1))
    return xt.reshape(batch, seq, d)
```

```python
import functools

import jax
import jax.numpy as jnp
from jax import lax
from jax.experimental import pallas as pl
from jax.experimental.pallas import tpu as pltpu

F32 = jnp.float32
BF16 = jnp.bfloat16

EPS = 1e-6
NEG_INF = -1e30
HEAD_DIM = 64
FOX_HEADS = 8
SWA_Q_HEADS = 8
SWA_KV_HEADS = 2
WINDOW = 128
GLA_HEADS = 4
GLA_HEAD_K = 128
GLA_HEAD_V = 256
GLA_RANK = 16
GLA_TAU = 16.0
GLA_CHUNK = 64
GLA_SUPER = 256
FOX_Q_TILE = 512
SWA_UNROLL = 3
LANES = 128
VMEM_LIMIT_BYTES = 56 * 1024 * 1024

RAW_FQ, RAW_FK, RAW_FV = 0, 512, 1024
RAW_SQ, RAW_SK, RAW_SV = 1536, 2048, 2304
RAW_GQ, RAW_GK, RAW_GV = 2560, 3072, 3584
RAW_WIDTH = 4608
ACT_WIDTH = 4096
SMALL_WIDTH = 128


def _params(n_grid_axes):
    return pltpu.CompilerParams(
        dimension_semantics=("arbitrary",) * n_grid_axes,
        vmem_limit_bytes=VMEM_LIMIT_BYTES)


def _resident(shape):
    nd = len(shape)
    return pl.BlockSpec(shape, lambda *_: (0,) * nd, pipeline_mode=pl.Buffered(1))


def _layer(shape, layer):
    nd = len(shape)
    return pl.BlockSpec((None,) + tuple(shape), lambda *_: (layer,) + (0,) * nd,
                        pipeline_mode=pl.Buffered(1))


def _rmsnorm(x, w):
    return x * lax.rsqrt(jnp.mean(x * x, axis=-1, keepdims=True) + EPS) * w


def _log_sigmoid(x):
    return jnp.minimum(x, 0.0) - jnp.log1p(jnp.exp(-jnp.abs(x)))


def _split3(x):
    p1 = x.astype(BF16)
    r1 = x - p1.astype(F32)
    p2 = r1.astype(BF16)
    p3 = (r1 - p2.astype(F32)).astype(BF16)
    return p1, p2, p3


def _dot(a, b):
    return jnp.dot(a, b, preferred_element_type=F32)


def _dot_nt(a, b):
    return lax.dot_general(a, b, (((1,), (1,)), ((), ())), preferred_element_type=F32)


def _dot_tn(a, b):
    return lax.dot_general(a, b, (((0,), (0,)), ((), ())), preferred_element_type=F32)


def _exact_tri_cumsum(tri, x):
    p1, p2, p3 = _split3(x)
    return _dot(tri, p1) + _dot(tri, p2) + _dot(tri, p3)


def _ffn_kernel(x_ref, nw_ref, wgu_ref, wd_ref, fw_ref, o_ref, h_ref, a_ref, *,
                f_chunk, final_norm):
    h_ref[...] = _rmsnorm(x_ref[...], nw_ref[...]).astype(BF16)
    d_ff = wd_ref.shape[0]
    for c in range(d_ff // f_chunk):
        sl = slice(c * f_chunk, (c + 1) * f_chunk)
        g = _dot(h_ref[...], wgu_ref[:, sl])
        u = _dot(h_ref[...], wgu_ref[:, d_ff + c * f_chunk:d_ff + (c + 1) * f_chunk])
        a_ref[:, sl] = (g * jax.nn.sigmoid(g) * u).astype(BF16)
    y = x_ref[...] + 0.5 * _dot(a_ref[...], wd_ref[...])
    if final_norm:
        y = _rmsnorm(y, fw_ref[...])
    o_ref[...] = y


def _ffn(x, nw, wgu, wd, fw, *, layer, final_norm, tm=512, f_chunk=256):
    t, d = x.shape
    d_ff = wd.shape[1]
    row = pl.BlockSpec((tm, d), lambda i: (i, 0))
    return pl.pallas_call(
        functools.partial(_ffn_kernel, f_chunk=f_chunk, final_norm=final_norm),
        grid=(t // tm,),
        in_specs=[row, _resident((1, d)), _layer((d, 2 * d_ff), layer),
                  _layer((d_ff, d), layer), _resident((1, d))],
        out_specs=row,
        out_shape=jax.ShapeDtypeStruct((t, d), F32),
        scratch_shapes=[pltpu.VMEM((tm, d), BF16), pltpu.VMEM((tm, d_ff), BF16)],
        compiler_params=_params(1),
        name="ffn",
    )(x, nw, wgu, wd, fw)


def _inproj_kernel(x_ref, nw_ref, wraw_ref, wact_ref, wsm_ref, w2_ref, fb_ref, gb_ref,
                   raw_ref, act_ref, logf_ref, bcum_ref, h_ref, *, n_chunk, gr_width):
    tm = x_ref.shape[0]
    h_ref[...] = _rmsnorm(x_ref[...], nw_ref[...]).astype(BF16)
    for c in range(raw_ref.shape[1] // n_chunk):
        sl = slice(c * n_chunk, (c + 1) * n_chunk)
        raw_ref[:, sl] = _dot(h_ref[...], wraw_ref[:, sl]).astype(BF16)
    for c in range(act_ref.shape[1] // n_chunk):
        sl = slice(c * n_chunk, (c + 1) * n_chunk)
        z = _dot(h_ref[...], wact_ref[:, sl])
        if c * n_chunk < gr_width:
            act_ref[:, sl] = (z * jax.nn.sigmoid(z)).astype(BF16)
        else:
            act_ref[:, sl] = jax.nn.sigmoid(z).astype(BF16)
    small = _dot(h_ref[...], wsm_ref[...])
    logf_ref[...] = _log_sigmoid(small + fb_ref[...])
    gate_logit = _dot(small.astype(BF16), w2_ref[...]) + gb_ref[...]
    log_a = _log_sigmoid(gate_logit) * (1.0 / GLA_TAU)
    r = lax.broadcasted_iota(jnp.int32, (GLA_SUPER, GLA_SUPER), 0)
    c = lax.broadcasted_iota(jnp.int32, (GLA_SUPER, GLA_SUPER), 1)
    tri = jnp.where((r >= c) & ((r // GLA_CHUNK) == (c // GLA_CHUNK)), 1.0, 0.0).astype(BF16)
    for s in range(tm // GLA_SUPER):
        rows = slice(s * GLA_SUPER, (s + 1) * GLA_SUPER)
        bcum_ref[rows, :] = _exact_tri_cumsum(tri, log_a[rows, :])


def _inproj(x, nw, w_raw, w_act, w_small, w2p, fbias, gbias, *, layer, tm=512, n_chunk=512):
    t, d = x.shape
    gk = w2p.shape[2]
    row = lambda w: pl.BlockSpec((tm, w), lambda i: (i, 0))
    return pl.pallas_call(
        functools.partial(_inproj_kernel, n_chunk=n_chunk, gr_width=GLA_HEADS * GLA_HEAD_V),
        grid=(t // tm,),
        in_specs=[row(d), _resident((1, d)), _layer((d, RAW_WIDTH), layer),
                  _layer((d, ACT_WIDTH), layer), _layer((d, SMALL_WIDTH), layer),
                  _layer((SMALL_WIDTH, gk), layer),
                  _resident((1, SMALL_WIDTH)), _resident((1, gk))],
        out_specs=[row(RAW_WIDTH), row(ACT_WIDTH), row(SMALL_WIDTH), row(gk)],
        out_shape=[jax.ShapeDtypeStruct((t, RAW_WIDTH), BF16),
                   jax.ShapeDtypeStruct((t, ACT_WIDTH), BF16),
                   jax.ShapeDtypeStruct((t, SMALL_WIDTH), F32),
                   jax.ShapeDtypeStruct((t, gk), F32)],
        scratch_shapes=[pltpu.VMEM((tm, d), BF16)],
        compiler_params=_params(1),
        name="inproj",
    )(x, nw, w_raw, w_act, w_small, w2p, fbias, gbias)


def _fox_kernel(q_ref, k_ref, v_ref, logf_ref, o_ref,
                c_ref, qa_ref, qb_ref, ka_ref, kb_ref, va_ref, vb_ref, m_ref, acc_ref, *, tq):
    s_len = q_ref.shape[0]
    pair = pl.program_id(1)
    blk = LANES
    log2e = 1.4426950408889634

    @pl.when(pair == 0)
    def _():
        r = lax.broadcasted_iota(jnp.int32, (blk, blk), 0)
        cc = lax.broadcasted_iota(jnp.int32, (blk, blk), 1)
        tri = jnp.where(r >= cc, 1.0, 0.0).astype(BF16)
        carry = jnp.zeros((1, LANES), F32)
        for n in range(s_len // blk):
            rows = slice(n * blk, (n + 1) * blk)
            within = _exact_tri_cumsum(tri, logf_ref[rows, :]) + carry
            c_ref[rows, :] = within * log2e
            carry = within[blk - 1:blk, :]

    lane = lax.broadcasted_iota(jnp.int32, (1, LANES), 1)
    low = lane < HEAD_DIM
    qf = q_ref[...].astype(F32) * (HEAD_DIM ** -0.5 * log2e)
    kf = k_ref[...].astype(F32)
    v = v_ref[...]
    c_all = c_ref[...]
    one = jnp.ones((1, LANES), F32)
    zero = jnp.zeros((1, LANES), F32)
    for head, (q_out, k_out, v_out) in enumerate(((qa_ref, ka_ref, va_ref),
                                                   (qb_ref, kb_ref, vb_ref))):
        own = low if head == 0 else jnp.logical_not(low)
        aug = lane - HEAD_DIM if head == 0 else lane
        ch = jnp.sum(jnp.where(lane == 2 * pair + head, c_all, 0.0), axis=-1, keepdims=True)
        c1, c2, c3 = (p.astype(F32) for p in _split3(ch))
        ones_q = jnp.where((aug >= 3) & (aug < 6), one, zero)
        ones_k = jnp.where((aug >= 0) & (aug < 3), one, zero)
        q_aug = jnp.where(aug == 0, c1, jnp.where(aug == 1, c2, jnp.where(aug == 2, c3, ones_q)))
        k_aug = jnp.where(aug == 3, -c1, jnp.where(aug == 4, -c2, jnp.where(aug == 5, -c3, ones_k)))
        q_out[...] = jnp.where(own, qf, q_aug).astype(BF16)
        k_out[...] = jnp.where(own, kf, k_aug).astype(BF16)
        v_out[...] = jnp.where(own, v, jnp.where(aug == 0, one, zero).astype(BF16))

    heads = ((qa_ref, ka_ref, va_ref), (qb_ref, kb_ref, vb_ref))

    def attend(h, q0, nq, k_rows, bias):
        q_ref_h, k_ref_h, v_ref_h = heads[h]
        t_rows = slice(q0 % tq, q0 % tq + nq)
        s = _dot_nt(q_ref_h[q0:q0 + nq, :], k_ref_h[k_rows, :])
        if bias is not None:
            s = s + bias
        m_prev = m_ref[h, t_rows, :]
        m_new = jnp.maximum(m_prev, jnp.max(s, axis=-1, keepdims=True))
        alpha = jnp.exp2(m_prev - m_new)
        p = jnp.exp2(s - pltpu.repeat(m_new, s.shape[1] // LANES, axis=1))
        acc_ref[h, t_rows, :] = (alpha * acc_ref[h, t_rows, :]
                                 + _dot(p.astype(BF16), v_ref_h[k_rows, :]))
        m_ref[h, t_rows, :] = m_new

    half = tq // 2
    ri = lax.broadcasted_iota(jnp.int32, (half, half), 0)
    ci = lax.broadcasted_iota(jnp.int32, (half, half), 1)
    causal_bias = jnp.where(ci <= ri, 0.0, NEG_INF)
    sum_lane_a = lane == HEAD_DIM
    sum_lane_b = lane == 0

    for qi in range(s_len // tq):
        q0 = qi * tq
        q_rows = slice(q0, q0 + tq)
        m_ref[...] = jnp.full(m_ref.shape, NEG_INF, F32)
        acc_ref[...] = jnp.zeros(acc_ref.shape, F32)

        def kv_step(kj, _):
            k_rows = pl.ds(pl.multiple_of(kj * tq, tq), tq)
            for h in range(2):
                attend(h, q0, tq, k_rows, None)
            return 0

        if qi > 0:
            lax.fori_loop(0, qi, kv_step, 0)
        for h in range(2):
            attend(h, q0 + half, half, slice(q0, q0 + half), None)
            attend(h, q0, half, slice(q0, q0 + half), causal_bias)
            attend(h, q0 + half, half, slice(q0 + half, q0 + tq), causal_bias)
        acc_a, acc_b = acc_ref[0], acc_ref[1]
        l_a = jnp.sum(jnp.where(sum_lane_a, acc_a, 0.0), axis=-1, keepdims=True)
        l_b = jnp.sum(jnp.where(sum_lane_b, acc_b, 0.0), axis=-1, keepdims=True)
        o_ref[q_rows, :] = jnp.where(low, acc_a / l_a, acc_b / l_b).astype(BF16)


def _fox(raw, logf, *, batch, seq):
    t = raw.shape[0]
    tq = min(FOX_Q_TILE, seq)
    n_pairs = FOX_HEADS // 2
    blk = lambda off: pl.BlockSpec((seq, LANES), lambda b, j: (b, off // LANES + j))
    return pl.pallas_call(
        functools.partial(_fox_kernel, tq=tq),
        grid=(batch, n_pairs),
        in_specs=[blk(RAW_FQ), blk(RAW_FK), blk(RAW_FV),
                  pl.BlockSpec((seq, LANES), lambda b, j: (b, 0))],
        out_specs=pl.BlockSpec((seq, LANES), lambda b, j: (b, j)),
        out_shape=jax.ShapeDtypeStruct((t, FOX_HEADS * HEAD_DIM), BF16),
        scratch_shapes=[pltpu.VMEM((seq, LANES), F32)]
                       + [pltpu.VMEM((seq, LANES), BF16)] * 6
                       + [pltpu.VMEM((2, tq, LANES), F32), pltpu.VMEM((2, tq, LANES), F32)],
        compiler_params=_params(2),
        name="fox",
    )(raw, raw, raw, logf)


def _swa_kernel(sink_ref, q_ref, k_ref, v_ref, o_ref):
    s_len = q_ref.shape[0]
    w = WINDOW
    group = SWA_Q_HEADS // SWA_KV_HEADS
    log2e = 1.4426950408889634
    scale = HEAD_DIM ** -0.5 * log2e
    low = lax.broadcasted_iota(jnp.int32, (w, LANES), 1) < HEAD_DIM
    i = lax.broadcasted_iota(jnp.int32, (w, 2 * w), 0)
    j = lax.broadcasted_iota(jnp.int32, (w, 2 * w), 1)
    band = jnp.where((j > i) & (j <= i + w), 0.0, NEG_INF)
    band = jnp.concatenate([band] * group, axis=0)
    it = lax.broadcasted_iota(jnp.int32, (w, w), 0)
    jt = lax.broadcasted_iota(jnp.int32, (w, w), 1)
    tri = jnp.concatenate([jnp.where(jt <= it, 0.0, NEG_INF)] * group, axis=0)
    hrow = lax.broadcasted_iota(jnp.int32, (group * w, LANES), 0) // w

    def block(n, kvh, sink, first):
        q_rows = pl.ds(pl.multiple_of(n * w, w), w)
        qs = []
        for p in range(group // 2):
            qp = q_ref[q_rows, (kvh * group // 2 + p) * LANES:(kvh * group // 2 + p + 1) * LANES]
            zero = jnp.zeros_like(qp)
            qs += [jnp.where(low, qp, zero), jnp.where(low, zero, qp)]
        q = jnp.concatenate(qs, axis=0)
        k_rows = q_rows if first else pl.ds(pl.multiple_of((n - 1) * w, w), 2 * w)
        kv_cols = slice(kvh * LANES, (kvh + 1) * LANES)
        kk = k_ref[k_rows, kv_cols]
        vv = v_ref[k_rows, kv_cols]
        s = _dot_nt(q, kk) * scale + (tri if first else band)
        m = jnp.maximum(jnp.max(s, axis=-1, keepdims=True), sink)
        p = jnp.exp2(s - (m if first else pltpu.repeat(m, 2 * w // LANES, axis=1)))
        denom = jnp.sum(p, axis=-1, keepdims=True) + jnp.exp2(sink - m)
        o = _dot(p.astype(BF16), vv) / denom
        for pidx in range(group // 2):
            oa = o[(2 * pidx) * w:(2 * pidx + 1) * w, :]
            ob = o[(2 * pidx + 1) * w:(2 * pidx + 2) * w, :]
            col = (kvh * group // 2 + pidx) * LANES
            o_ref[q_rows, col:col + LANES] = jnp.where(low, oa, ob).astype(BF16)

    sinks = []
    for kvh in range(SWA_KV_HEADS):
        sink = jnp.zeros((group * w, LANES), F32)
        for g in range(group):
            sink = jnp.where(hrow == g, sink_ref[kvh * group + g] * log2e, sink)
        sinks.append(sink)
        block(0, kvh, sink, True)

    def body(n, _):
        for kvh in range(SWA_KV_HEADS):
            block(n, kvh, sinks[kvh], False)
        return 0

    lax.fori_loop(1, s_len // w, body, 0, unroll=SWA_UNROLL)


def _swa(raw, sinks, *, batch, seq):
    t = raw.shape[0]
    qw = SWA_Q_HEADS * HEAD_DIM
    kw = SWA_KV_HEADS * LANES
    return pl.pallas_call(
        _swa_kernel,
        grid=(batch,),
        in_specs=[pl.BlockSpec(memory_space=pltpu.SMEM),
                  pl.BlockSpec((seq, qw), lambda b: (b, RAW_SQ // qw)),
                  pl.BlockSpec((seq, kw), lambda b: (b, RAW_SK // kw)),
                  pl.BlockSpec((seq, kw), lambda b: (b, RAW_SV // kw))],
        out_specs=pl.BlockSpec((seq, qw), lambda b: (b, 0)),
        out_shape=jax.ShapeDtypeStruct((t, qw), BF16),
        compiler_params=_params(1),
        name="swa",
    )(sinks, raw, raw, raw)


def _gla_kernel(q_ref, k_ref, v_ref, b_ref, gr_ref, nw_ref, o_ref, st_ref):
    s_len = q_ref.shape[0]
    ck, sup = GLA_CHUNK, GLA_SUPER
    per = sup // ck
    scale = GLA_HEAD_K ** -0.5
    r = lax.broadcasted_iota(jnp.int32, (sup, sup), 0)
    c = lax.broadcasted_iota(jnp.int32, (sup, sup), 1)
    intra = (r >= c) & ((r // ck) == (c // ck))
    st_ref[...] = jnp.zeros(st_ref.shape, F32)

    def super_block(sb, _):
        rows = pl.ds(pl.multiple_of(sb * sup, sup), sup)
        b = b_ref[rows, :]
        b3 = b.reshape(per, ck, GLA_HEAD_K)
        b_last = b3[:, ck - 1:ck, :]
        qf = q_ref[rows, :].astype(F32) * scale
        kf = k_ref[rows, :].astype(F32)
        q_t = (qf * jnp.exp(b)).astype(BF16)
        k_t = (kf * jnp.exp(-b)).astype(BF16)
        k_s = (kf.reshape(per, ck, GLA_HEAD_K) * jnp.exp(b_last - b3)).astype(BF16)
        decay = jnp.exp(b_last)
        v = v_ref[rows, :]
        a = jnp.where(intra, _dot_nt(q_t, k_t), 0.0)
        o_intra = _dot(a.astype(BF16), v)
        st = st_ref[...]
        outs = []
        for cidx in range(per):
            chunk = slice(cidx * ck, (cidx + 1) * ck)
            outs.append(o_intra[chunk, :] + _dot_nt(q_t[chunk, :], st.astype(BF16)))
            st = st * decay[cidx] + _dot_tn(v[chunk, :], k_s[cidx])
        st_ref[...] = st
        o = jnp.concatenate(outs, axis=0)
        o = o * lax.rsqrt(jnp.mean(o * o, axis=-1, keepdims=True) + EPS) * nw_ref[...]
        o_ref[rows, :] = (o * gr_ref[rows, :].astype(F32)).astype(BF16)
        return 0

    lax.fori_loop(0, s_len // sup, super_block, 0, unroll=2)


def _gla(raw, bcum, act, gla_norm, *, batch, seq):
    t = raw.shape[0]
    n_chunks = seq // GLA_CHUNK
    dk, dv = GLA_HEAD_K, GLA_HEAD_V
    return pl.pallas_call(
        _gla_kernel,
        grid=(batch, GLA_HEADS),
        in_specs=[pl.BlockSpec((seq, dk), lambda b, h: (b, RAW_GQ // dk + h)),
                  pl.BlockSpec((seq, dk), lambda b, h: (b, RAW_GK // dk + h)),
                  pl.BlockSpec((seq, dv), lambda b, h: (b, RAW_GV // dv + h)),
                  pl.BlockSpec((seq, dk), lambda b, h: (b, h)),
                  pl.BlockSpec((seq, dv), lambda b, h: (b, h)),
                  pl.BlockSpec((1, dv), lambda b, h: (0, 0))],
        out_specs=pl.BlockSpec((seq, dv), lambda b, h: (b, h)),
        out_shape=jax.ShapeDtypeStruct((t, GLA_HEADS * dv), BF16),
        scratch_shapes=[pltpu.VMEM((dv, dk), F32)],
        compiler_params=_params(2),
        name="gla",
    )(raw, raw, raw, bcum, act, gla_norm)


def _merge_kernel(x_ref, of_ref, os_ref, og_ref, g0_ref, g1_ref, g2_ref,
                  wf_ref, ws_ref, wg_ref, wo_ref, o_ref):
    merged = (g0_ref[...].astype(F32) * _dot(of_ref[...], wf_ref[...])
              + g1_ref[...].astype(F32) * _dot(os_ref[...], ws_ref[...])
              + g2_ref[...].astype(F32) * _dot(og_ref[...], wg_ref[...]))
    o_ref[...] = x_ref[...] + _dot(merged.astype(BF16), wo_ref[...])


def _merge(x, o_fox, o_swa, o_gla, act, wf, ws, wg, wo, *, layer, tm=512):
    t, d = x.shape
    row = lambda w, j=0: pl.BlockSpec((tm, w), lambda i: (i, j))
    return pl.pallas_call(
        _merge_kernel,
        grid=(t // tm,),
        in_specs=[row(d), row(o_fox.shape[1]), row(o_swa.shape[1]), row(o_gla.shape[1]),
                  row(d, 1), row(d, 2), row(d, 3),
                  _layer(wf.shape[1:], layer), _layer(ws.shape[1:], layer),
                  _layer(wg.shape[1:], layer), _layer(wo.shape[1:], layer)],
        out_specs=row(d),
        out_shape=jax.ShapeDtypeStruct((t, d), F32),
        compiler_params=_params(1),
        name="merge",
    )(x, o_fox, o_swa, o_gla, act, act, act, wf, ws, wg, wo)


def _prep_w_in(w_in):
    w = w_in.astype(BF16)
    sizes = (512, 512, 512, FOX_HEADS, 512, 128, 128, 512, 512, 1024, GLA_RANK, 1024, 3072)
    offs = [0]
    for s in sizes:
        offs.append(offs[-1] + s)
    (fq, fk, fv, ff, sq, sk, sv, gq, gk, gv, glr, gr, gates) = (
        w[..., offs[i]:offs[i + 1]] for i in range(len(sizes)))

    def dup_heads(m):
        return jnp.concatenate([m[..., :HEAD_DIM], m[..., :HEAD_DIM],
                                m[..., HEAD_DIM:], m[..., HEAD_DIM:]], axis=-1)

    w_raw = jnp.concatenate([fq, fk, fv, sq, dup_heads(sk), dup_heads(sv), gq, gk, gv], axis=-1)
    w_act = jnp.concatenate([gr, gates], axis=-1)
    pad = jnp.zeros(w.shape[:-1] + (SMALL_WIDTH - FOX_HEADS - GLA_RANK,), BF16)
    w_small = jnp.concatenate([ff, glr, pad], axis=-1)
    return w_raw, w_act, w_small


def kernel(x, ffn1_norm, ffn1_w_gu, ffn1_w_down, mix_norm, w_in, fox_forget_bias, swa_sinks,
           gla_gate_w2, gla_gate_bias, gla_norm, w_branch_fox, w_branch_swa, w_branch_gla, w_out,
           ffn2_norm, ffn2_w_gu, ffn2_w_down, final_norm):
    batch, seq, d = x.shape
    depth = w_in.shape[0]
    xt = x.reshape(batch * seq, d)

    w_raw, w_act, w_small = _prep_w_in(w_in)
    w2p = jnp.zeros((depth, SMALL_WIDTH, gla_gate_w2.shape[-1]), BF16)
    w2p = w2p.at[:, FOX_HEADS:FOX_HEADS + GLA_RANK, :].set(gla_gate_w2.astype(BF16))
    fbias = jnp.zeros((depth, 1, SMALL_WIDTH), F32).at[:, 0, :FOX_HEADS].set(fox_forget_bias)
    f1gu, f1d = ffn1_w_gu.astype(BF16), ffn1_w_down.astype(BF16)
    f2gu, f2d = ffn2_w_gu.astype(BF16), ffn2_w_down.astype(BF16)
    wbf, wbs = w_branch_fox.astype(BF16), w_branch_swa.astype(BF16)
    wbg, wo = w_branch_gla.astype(BF16), w_out.astype(BF16)
    fw = final_norm.reshape(1, d)

    for l in range(depth):
        xt = _ffn(xt, ffn1_norm[l].reshape(1, d), f1gu, f1d, fw, layer=l, final_norm=False)
        raw, act, logf, bcum = _inproj(xt, mix_norm[l].reshape(1, d), w_raw, w_act, w_small, w2p,
                                       fbias[l], gla_gate_bias[l].reshape(1, -1), layer=l)
        o_fox = _fox(raw, logf, batch=batch, seq=seq)
        o_swa = _swa(raw, swa_sinks[l], batch=batch, seq=seq)
        o_gla = _gla(raw, bcum, act, gla_norm[l].reshape(1, -1), batch=batch, seq=seq)
        xt = _merge(xt, o_fox, o_swa, o_gla, act, wbf, wbs, wbg, wo, layer=l)
        xt = _ffn(xt, ffn2_norm[l].reshape(1, d), f2gu, f2d, fw, layer=l,
                  final_norm=(l == depth - 1))
    return xt.reshape(batch, seq, d)
```

```python
import functools

import jax
import jax.numpy as jnp
from jax import lax
from jax.experimental import pallas as pl
from jax.experimental.pallas import tpu as pltpu

F32 = jnp.float32
BF16 = jnp.bfloat16

EPS = 1e-6
NEG_INF = -1e30
HEAD_DIM = 64
FOX_HEADS = 8
SWA_Q_HEADS = 8
SWA_KV_HEADS = 2
WINDOW = 128
GLA_HEADS = 4
GLA_HEAD_K = 128
GLA_HEAD_V = 256
GLA_RANK = 16
GLA_TAU = 16.0
GLA_CHUNK = 64
GLA_SUPER = 256
FOX_Q_TILE = 1024
FOX_Q_SCALE = HEAD_DIM ** -0.5 * 1.4426950408889634
SWA_UNROLL = 3
LANES = 128
VMEM_LIMIT_BYTES = 56 * 1024 * 1024

RAW_FQ, RAW_FK, RAW_FV = 0, 512, 1024
RAW_SQ, RAW_SK, RAW_SV = 1536, 2048, 2304
RAW_GQ, RAW_GK, RAW_GV = 2560, 3072, 3584
RAW_WIDTH = 4608
ACT_WIDTH = 4096
SMALL_WIDTH = 128


def _params(n_grid_axes):
    return pltpu.CompilerParams(
        dimension_semantics=("arbitrary",) * n_grid_axes,
        vmem_limit_bytes=VMEM_LIMIT_BYTES)


def _resident(shape):
    nd = len(shape)
    return pl.BlockSpec(shape, lambda *_: (0,) * nd, pipeline_mode=pl.Buffered(1))


def _layer(shape, layer):
    nd = len(shape)
    return pl.BlockSpec((None,) + tuple(shape), lambda *_: (layer,) + (0,) * nd,
                        pipeline_mode=pl.Buffered(1))


def _rmsnorm(x, w):
    return x * lax.rsqrt(jnp.mean(x * x, axis=-1, keepdims=True) + EPS) * w


def _log_sigmoid(x):
    return jnp.minimum(x, 0.0) - jnp.log1p(jnp.exp(-jnp.abs(x)))


def _split3(x):
    p1 = x.astype(BF16)
    r1 = x - p1.astype(F32)
    p2 = r1.astype(BF16)
    p3 = (r1 - p2.astype(F32)).astype(BF16)
    return p1, p2, p3


def _lane_repeat(x, n):
    return jnp.concatenate([x] * n, axis=1)


def _dot(a, b):
    return jnp.dot(a, b, preferred_element_type=F32)


def _dot_nt(a, b):
    return lax.dot_general(a, b, (((1,), (1,)), ((), ())), preferred_element_type=F32)


def _dot_tn(a, b):
    return lax.dot_general(a, b, (((0,), (0,)), ((), ())), preferred_element_type=F32)


def _exact_tri_cumsum(tri, x):
    p1, p2, p3 = _split3(x)
    return _dot(tri, p1) + _dot(tri, p2) + _dot(tri, p3)


def _ffn_kernel(x_ref, nw_ref, wgu_ref, wd_ref, fw_ref, o_ref, h_ref, a_ref, *,
                f_chunk, final_norm):
    h_ref[...] = _rmsnorm(x_ref[...], nw_ref[...]).astype(BF16)
    d_ff = wd_ref.shape[0]
    for c in range(d_ff // f_chunk):
        sl = slice(c * f_chunk, (c + 1) * f_chunk)
        g = _dot(h_ref[...], wgu_ref[:, sl])
        u = _dot(h_ref[...], wgu_ref[:, d_ff + c * f_chunk:d_ff + (c + 1) * f_chunk])
        a_ref[:, sl] = (g * jax.nn.sigmoid(g) * u).astype(BF16)
    y = x_ref[...] + 0.5 * _dot(a_ref[...], wd_ref[...])
    if final_norm:
        y = _rmsnorm(y, fw_ref[...])
    o_ref[...] = y


def _ffn(x, nw, wgu, wd, fw, *, layer, final_norm, tm=1024, f_chunk=256):
    t, d = x.shape
    d_ff = wd.shape[1]
    row = pl.BlockSpec((tm, d), lambda i: (i, 0))
    return pl.pallas_call(
        functools.partial(_ffn_kernel, f_chunk=f_chunk, final_norm=final_norm),
        grid=(t // tm,),
        in_specs=[row, _resident((1, d)), _layer((d, 2 * d_ff), layer),
                  _layer((d_ff, d), layer), _resident((1, d))],
        out_specs=row,
        out_shape=jax.ShapeDtypeStruct((t, d), F32),
        scratch_shapes=[pltpu.VMEM((tm, d), BF16), pltpu.VMEM((tm, d_ff), BF16)],
        compiler_params=_params(1),
        name="ffn",
    )(x, nw, wgu, wd, fw)


def _inproj_kernel(x_ref, nw_ref, wraw_ref, wact_ref, wsm_ref, w2_ref, fb_ref, gb_ref,
                   raw_ref, act_ref, logf_ref, bcum_ref, h_ref, *, n_chunk, gr_width):
    tm = x_ref.shape[0]
    h_ref[...] = _rmsnorm(x_ref[...], nw_ref[...]).astype(BF16)
    for c in range(raw_ref.shape[1] // n_chunk):
        sl = slice(c * n_chunk, (c + 1) * n_chunk)
        z = _dot(h_ref[...], wraw_ref[:, sl])
        if sl.stop <= RAW_FK:
            z = z * FOX_Q_SCALE
        raw_ref[:, sl] = z.astype(BF16)
    for c in range(act_ref.shape[1] // n_chunk):
        sl = slice(c * n_chunk, (c + 1) * n_chunk)
        z = _dot(h_ref[...], wact_ref[:, sl])
        if c * n_chunk < gr_width:
            act_ref[:, sl] = (z * jax.nn.sigmoid(z)).astype(BF16)
        else:
            act_ref[:, sl] = jax.nn.sigmoid(z).astype(BF16)
    small = _dot(h_ref[...], wsm_ref[...])
    logf_ref[...] = _log_sigmoid(small + fb_ref[...])
    gate_logit = _dot(small.astype(BF16), w2_ref[...]) + gb_ref[...]
    log_a = _log_sigmoid(gate_logit) * (1.0 / GLA_TAU)
    r = lax.broadcasted_iota(jnp.int32, (GLA_SUPER, GLA_SUPER), 0)
    c = lax.broadcasted_iota(jnp.int32, (GLA_SUPER, GLA_SUPER), 1)
    tri = jnp.where((r >= c) & ((r // GLA_CHUNK) == (c // GLA_CHUNK)), 1.0, 0.0).astype(BF16)
    for s in range(tm // GLA_SUPER):
        rows = slice(s * GLA_SUPER, (s + 1) * GLA_SUPER)
        bcum_ref[rows, :] = _exact_tri_cumsum(tri, log_a[rows, :])


def _inproj(x, nw, w_raw, w_act, w_small, w2p, fbias, gbias, *, layer, tm=512, n_chunk=512):
    t, d = x.shape
    gk = w2p.shape[2]
    row = lambda w: pl.BlockSpec((tm, w), lambda i: (i, 0))
    return pl.pallas_call(
        functools.partial(_inproj_kernel, n_chunk=n_chunk, gr_width=GLA_HEADS * GLA_HEAD_V),
        grid=(t // tm,),
        in_specs=[row(d), _resident((1, d)), _layer((d, RAW_WIDTH), layer),
                  _layer((d, ACT_WIDTH), layer), _layer((d, SMALL_WIDTH), layer),
                  _layer((SMALL_WIDTH, gk), layer),
                  _resident((1, SMALL_WIDTH)), _resident((1, gk))],
        out_specs=[row(RAW_WIDTH), row(ACT_WIDTH), row(SMALL_WIDTH), row(gk)],
        out_shape=[jax.ShapeDtypeStruct((t, RAW_WIDTH), BF16),
                   jax.ShapeDtypeStruct((t, ACT_WIDTH), BF16),
                   jax.ShapeDtypeStruct((t, SMALL_WIDTH), F32),
                   jax.ShapeDtypeStruct((t, gk), F32)],
        scratch_shapes=[pltpu.VMEM((tm, d), BF16)],
        compiler_params=_params(1),
        name="inproj",
    )(x, nw, w_raw, w_act, w_small, w2p, fbias, gbias)


def _fox_kernel(q_ref, k_ref, v_ref, logf_ref, o_ref,
                p_ref, qa_ref, qb_ref, ka_ref, kb_ref, va_ref, vb_ref, m_ref, acc_ref, *, tq):
    s_len = q_ref.shape[0]
    pair = pl.program_id(1)
    blk = LANES
    log2e = 1.4426950408889634
    n_heads = FOX_HEADS
    one_lane = 3 * n_heads
    lane = lax.broadcasted_iota(jnp.int32, (1, LANES), 1)

    @pl.when(pair == 0)
    def _():
        r = lax.broadcasted_iota(jnp.int32, (blk, blk), 0)
        cc = lax.broadcasted_iota(jnp.int32, (blk, blk), 1)
        tri = jnp.where(r >= cc, 1.0, 0.0).astype(BF16)
        ones = jnp.where(lane == one_lane, 1.0, 0.0)
        carry = jnp.zeros((1, LANES), F32)
        for n in range(s_len // blk):
            rows = slice(n * blk, (n + 1) * blk)
            within = _exact_tri_cumsum(tri, logf_ref[rows, :]) + carry
            carry = within[blk - 1:blk, :]
            p1, p2, p3 = (p.astype(F32) for p in _split3(jnp.where(lane < n_heads,
                                                                    within * log2e, 0.0)))
            packed = (p1 + pltpu.roll(p2, n_heads, axis=1) + pltpu.roll(p3, 2 * n_heads, axis=1)
                      + ones)
            p_ref[rows, :] = packed.astype(BF16)

    row = lax.broadcasted_iota(jnp.int32, (LANES, 2 * LANES), 0)
    col = lax.broadcasted_iota(jnp.int32, (LANES, 2 * LANES), 1)
    k_side = col >= LANES
    dst = col % LANES
    aug = dst % HEAD_DIM
    head = 2 * pair + jnp.where(dst < HEAD_DIM, 1, 0)
    part = aug - jnp.where(k_side, 3, 0)
    from_c = (part >= 0) & (part < 3) & (row == head + n_heads * part)
    ones_pos = aug - jnp.where(k_side, 0, 3)
    from_one = (row == one_lane) & (ones_pos >= 0) & (ones_pos < 3)
    sel = jnp.where(from_c, jnp.where(k_side, -1.0, 1.0), jnp.where(from_one, 1.0, 0.0))
    spare = _dot(p_ref[...], sel.astype(BF16))
    q_spare = spare[:, :LANES].astype(BF16)
    k_spare = spare[:, LANES:].astype(BF16)

    low = lax.broadcasted_iota(jnp.int32, (s_len, LANES), 1) < HEAD_DIM
    q, k, v = q_ref[...], k_ref[...], v_ref[...]
    qa_ref[...] = jnp.where(low, q, q_spare)
    qb_ref[...] = jnp.where(low, q_spare, q)
    ka_ref[...] = jnp.where(low, k, k_spare)
    kb_ref[...] = jnp.where(low, k_spare, k)
    va_ref[...] = jnp.where(low, v, jnp.where(lane == HEAD_DIM, 1.0, 0.0).astype(BF16))
    vb_ref[...] = jnp.where(low, jnp.where(lane == 0, 1.0, 0.0).astype(BF16), v)

    heads = ((qa_ref, ka_ref, va_ref), (qb_ref, kb_ref, vb_ref))

    def scores(h, q0, nq, k_rows, bias):
        q_ref_h, k_ref_h, _ = heads[h]
        s = _dot_nt(q_ref_h[q0:q0 + nq, :], k_ref_h[k_rows, :])
        return s if bias is None else s + bias

    def start(h, q0, nq, k_rows, bias):
        s = scores(h, q0, nq, k_rows, bias)
        m = jnp.max(s, axis=-1, keepdims=True) + jnp.zeros((nq, LANES), F32)
        p = jnp.exp2(s - _lane_repeat(m, s.shape[1] // LANES))
        acc_ref[h, q0:q0 + nq, :] = _dot(p.astype(BF16), heads[h][2][k_rows, :])
        m_ref[h, q0:q0 + nq, :] = m

    def update(h, q0, nq, k_rows):
        s = scores(h, q0, nq, k_rows, None)
        m_prev = m_ref[h, q0:q0 + nq, :]
        m_new = jnp.maximum(m_prev, jnp.max(s, axis=-1, keepdims=True))
        alpha = jnp.exp2(m_prev - m_new)
        p = jnp.exp2(s - _lane_repeat(m_new, s.shape[1] // LANES))
        acc_ref[h, q0:q0 + nq, :] = (alpha * acc_ref[h, q0:q0 + nq, :]
                                     + _dot(p.astype(BF16), heads[h][2][k_rows, :]))
        m_ref[h, q0:q0 + nq, :] = m_new

    half = tq // 2
    ri = lax.broadcasted_iota(jnp.int32, (half, tq), 0)
    ci = lax.broadcasted_iota(jnp.int32, (half, tq), 1)
    upper_bias = jnp.where(lax.broadcasted_iota(jnp.int32, (half, half), 1)
                           <= lax.broadcasted_iota(jnp.int32, (half, half), 0), 0.0, NEG_INF)
    lower_bias = jnp.where(ci <= ri + half, 0.0, NEG_INF)
    n_tiles = s_len // tq
    for qi in range(n_tiles):
        q0 = qi * tq
        for h in range(2):
            start(h, q0, half, slice(q0, q0 + half), upper_bias)
            start(h, q0 + half, half, slice(q0, q0 + tq), lower_bias)

    for qi in range(1, n_tiles):
        def kv_step(kj, _, q0=qi * tq):
            k_rows = pl.ds(pl.multiple_of(kj * tq, tq), tq)
            for h in range(2):
                update(h, q0, tq, k_rows)
            return 0

        lax.fori_loop(0, qi, kv_step, 0)

    acc_a, acc_b = acc_ref[0], acc_ref[1]
    l_a = jnp.sum(jnp.where(lane == HEAD_DIM, acc_a, 0.0), axis=-1, keepdims=True)
    l_b = jnp.sum(jnp.where(lane == 0, acc_b, 0.0), axis=-1, keepdims=True)
    o_ref[...] = jnp.where(lane < HEAD_DIM, acc_a / l_a, acc_b / l_b).astype(BF16)


def _fox(raw, logf, *, batch, seq):
    t = raw.shape[0]
    tq = min(FOX_Q_TILE, seq)
    n_pairs = FOX_HEADS // 2
    blk = lambda off: pl.BlockSpec((seq, LANES), lambda b, j: (b, off // LANES + j))
    return pl.pallas_call(
        functools.partial(_fox_kernel, tq=tq),
        grid=(batch, n_pairs),
        in_specs=[blk(RAW_FQ), blk(RAW_FK), blk(RAW_FV),
                  pl.BlockSpec((seq, LANES), lambda b, j: (b, 0))],
        out_specs=pl.BlockSpec((seq, LANES), lambda b, j: (b, j)),
        out_shape=jax.ShapeDtypeStruct((t, FOX_HEADS * HEAD_DIM), BF16),
        scratch_shapes=[pltpu.VMEM((seq, LANES), BF16)] * 7
                       + [pltpu.VMEM((2, seq, LANES), F32)] * 2,
        compiler_params=_params(2),
        name="fox",
    )(raw, raw, raw, logf)


def _swa_kernel(sink_ref, q_ref, k_ref, v_ref, o_ref):
    s_len = q_ref.shape[0]
    w = WINDOW
    group = SWA_Q_HEADS // SWA_KV_HEADS
    log2e = 1.4426950408889634
    scale = HEAD_DIM ** -0.5 * log2e
    low = lax.broadcasted_iota(jnp.int32, (w, LANES), 1) < HEAD_DIM
    i = lax.broadcasted_iota(jnp.int32, (w, 2 * w), 0)
    j = lax.broadcasted_iota(jnp.int32, (w, 2 * w), 1)
    band = jnp.where((j > i) & (j <= i + w), 0.0, NEG_INF)
    band = jnp.concatenate([band] * group, axis=0)
    it = lax.broadcasted_iota(jnp.int32, (w, w), 0)
    jt = lax.broadcasted_iota(jnp.int32, (w, w), 1)
    tri = jnp.concatenate([jnp.where(jt <= it, 0.0, NEG_INF)] * group, axis=0)
    hrow = lax.broadcasted_iota(jnp.int32, (group * w, LANES), 0) // w

    def block(n, kvh, sink, first):
        q_rows = pl.ds(pl.multiple_of(n * w, w), w)
        qs = []
        for p in range(group // 2):
            qp = q_ref[q_rows, (kvh * group // 2 + p) * LANES:(kvh * group // 2 + p + 1) * LANES]
            zero = jnp.zeros_like(qp)
            qs += [jnp.where(low, qp, zero), jnp.where(low, zero, qp)]
        q = jnp.concatenate(qs, axis=0)
        k_rows = q_rows if first else pl.ds(pl.multiple_of((n - 1) * w, w), 2 * w)
        kv_cols = slice(kvh * LANES, (kvh + 1) * LANES)
        kk = k_ref[k_rows, kv_cols]
        vv = v_ref[k_rows, kv_cols]
        s = _dot_nt(q, kk) * scale + (tri if first else band)
        m = jnp.maximum(jnp.max(s, axis=-1, keepdims=True), sink)
        p = jnp.exp2(s - (m if first else _lane_repeat(m, 2 * w // LANES)))
        denom = jnp.sum(p, axis=-1, keepdims=True) + jnp.exp2(sink - m)
        o = _dot(p.astype(BF16), vv) / denom
        for pidx in range(group // 2):
            oa = o[(2 * pidx) * w:(2 * pidx + 1) * w, :]
            ob = o[(2 * pidx + 1) * w:(2 * pidx + 2) * w, :]
            col = (kvh * group // 2 + pidx) * LANES
            o_ref[q_rows, col:col + LANES] = jnp.where(low, oa, ob).astype(BF16)

    sinks = []
    for kvh in range(SWA_KV_HEADS):
        sink = jnp.zeros((group * w, LANES), F32)
        for g in range(group):
            sink = jnp.where(hrow == g, sink_ref[kvh * group + g] * log2e, sink)
        sinks.append(sink)
        block(0, kvh, sink, True)

    def body(n, _):
        for kvh in range(SWA_KV_HEADS):
            block(n, kvh, sinks[kvh], False)
        return 0

    lax.fori_loop(1, s_len // w, body, 0, unroll=SWA_UNROLL)


def _swa(raw, sinks, *, batch, seq):
    t = raw.shape[0]
    qw = SWA_Q_HEADS * HEAD_DIM
    kw = SWA_KV_HEADS * LANES
    return pl.pallas_call(
        _swa_kernel,
        grid=(batch,),
        in_specs=[pl.BlockSpec(memory_space=pltpu.SMEM),
                  pl.BlockSpec((seq, qw), lambda b: (b, RAW_SQ // qw)),
                  pl.BlockSpec((seq, kw), lambda b: (b, RAW_SK // kw)),
                  pl.BlockSpec((seq, kw), lambda b: (b, RAW_SV // kw))],
        out_specs=pl.BlockSpec((seq, qw), lambda b: (b, 0)),
        out_shape=jax.ShapeDtypeStruct((t, qw), BF16),
        compiler_params=_params(1),
        name="swa",
    )(sinks, raw, raw, raw)


def _gla_kernel(q_ref, k_ref, v_ref, b_ref, gr_ref, nw_ref, o_ref, st_ref):
    s_len = q_ref.shape[0]
    ck, sup = GLA_CHUNK, GLA_SUPER
    per = sup // ck
    scale = GLA_HEAD_K ** -0.5
    r = lax.broadcasted_iota(jnp.int32, (sup, sup), 0)
    c = lax.broadcasted_iota(jnp.int32, (sup, sup), 1)
    intra = (r >= c) & ((r // ck) == (c // ck))
    st_ref[...] = jnp.zeros(st_ref.shape, F32)

    def super_block(sb, _):
        rows = pl.ds(pl.multiple_of(sb * sup, sup), sup)
        b = b_ref[rows, :]
        b3 = b.reshape(per, ck, GLA_HEAD_K)
        b_last = b3[:, ck - 1:ck, :]
        qf = q_ref[rows, :].astype(F32) * scale
        kf = k_ref[rows, :].astype(F32)
        q_t = (qf * jnp.exp(b)).astype(BF16)
        k_t = (kf * jnp.exp(-b)).astype(BF16)
        k_s = (kf.reshape(per, ck, GLA_HEAD_K) * jnp.exp(b_last - b3)).astype(BF16)
        decay = jnp.exp(b_last)
        v = v_ref[rows, :]
        a = jnp.where(intra, _dot_nt(q_t, k_t), 0.0)
        o_intra = _dot(a.astype(BF16), v)
        st = st_ref[...]
        outs = []
        for cidx in range(per):
            chunk = slice(cidx * ck, (cidx + 1) * ck)
            outs.append(o_intra[chunk, :] + _dot_nt(q_t[chunk, :], st.astype(BF16)))
            st = st * decay[cidx] + _dot_tn(v[chunk, :], k_s[cidx])
        st_ref[...] = st
        o = jnp.concatenate(outs, axis=0)
        o = o * lax.rsqrt(jnp.mean(o * o, axis=-1, keepdims=True) + EPS) * nw_ref[...]
        o_ref[rows, :] = (o * gr_ref[rows, :].astype(F32)).astype(BF16)
        return 0

    lax.fori_loop(0, s_len // sup, super_block, 0, unroll=2)


def _gla(raw, bcum, act, gla_norm, *, batch, seq):
    t = raw.shape[0]
    n_chunks = seq // GLA_CHUNK
    dk, dv = GLA_HEAD_K, GLA_HEAD_V
    return pl.pallas_call(
        _gla_kernel,
        grid=(batch, GLA_HEADS),
        in_specs=[pl.BlockSpec((seq, dk), lambda b, h: (b, RAW_GQ // dk + h)),
                  pl.BlockSpec((seq, dk), lambda b, h: (b, RAW_GK // dk + h)),
                  pl.BlockSpec((seq, dv), lambda b, h: (b, RAW_GV // dv + h)),
                  pl.BlockSpec((seq, dk), lambda b, h: (b, h)),
                  pl.BlockSpec((seq, dv), lambda b, h: (b, h)),
                  pl.BlockSpec((1, dv), lambda b, h: (0, 0))],
        out_specs=pl.BlockSpec((seq, dv), lambda b, h: (b, h)),
        out_shape=jax.ShapeDtypeStruct((t, GLA_HEADS * dv), BF16),
        scratch_shapes=[pltpu.VMEM((dv, dk), F32)],
        compiler_params=_params(2),
        name="gla",
    )(raw, raw, raw, bcum, act, gla_norm)


def _merge_kernel(x_ref, of_ref, os_ref, og_ref, g0_ref, g1_ref, g2_ref,
                  wf_ref, ws_ref, wg_ref, wo_ref, o_ref):
    merged = (g0_ref[...].astype(F32) * _dot(of_ref[...], wf_ref[...])
              + g1_ref[...].astype(F32) * _dot(os_ref[...], ws_ref[...])
              + g2_ref[...].astype(F32) * _dot(og_ref[...], wg_ref[...]))
    o_ref[...] = x_ref[...] + _dot(merged.astype(BF16), wo_ref[...])


def _merge(x, o_fox, o_swa, o_gla, act, wf, ws, wg, wo, *, layer, tm=1024):
    t, d = x.shape
    row = lambda w, j=0: pl.BlockSpec((tm, w), lambda i: (i, j))
    return pl.pallas_call(
        _merge_kernel,
        grid=(t // tm,),
        in_specs=[row(d), row(o_fox.shape[1]), row(o_swa.shape[1]), row(o_gla.shape[1]),
                  row(d, 1), row(d, 2), row(d, 3),
                  _layer(wf.shape[1:], layer), _layer(ws.shape[1:], layer),
                  _layer(wg.shape[1:], layer), _layer(wo.shape[1:], layer)],
        out_specs=row(d),
        out_shape=jax.ShapeDtypeStruct((t, d), F32),
        compiler_params=_params(1),
        name="merge",
    )(x, o_fox, o_swa, o_gla, act, act, act, wf, ws, wg, wo)


def _prep_w_in(w_in):
    w = w_in.astype(BF16)
    sizes = (512, 512, 512, FOX_HEADS, 512, 128, 128, 512, 512, 1024, GLA_RANK, 1024, 3072)
    offs = [0]
    for s in sizes:
        offs.append(offs[-1] + s)
    (fq, fk, fv, ff, sq, sk, sv, gq, gk, gv, glr, gr, gates) = (
        w[..., offs[i]:offs[i + 1]] for i in range(len(sizes)))

    def dup_heads(m):
        return jnp.concatenate([m[..., :HEAD_DIM], m[..., :HEAD_DIM],
                                m[..., HEAD_DIM:], m[..., HEAD_DIM:]], axis=-1)

    w_raw = jnp.concatenate([fq, fk, fv, sq, dup_heads(sk), dup_heads(sv), gq, gk, gv], axis=-1)
    w_act = jnp.concatenate([gr, gates], axis=-1)
    pad = jnp.zeros(w.shape[:-1] + (SMALL_WIDTH - FOX_HEADS - GLA_RANK,), BF16)
    w_small = jnp.concatenate([ff, glr, pad], axis=-1)
    return w_raw, w_act, w_small


def kernel(x, ffn1_norm, ffn1_w_gu, ffn1_w_down, mix_norm, w_in, fox_forget_bias, swa_sinks,
           gla_gate_w2, gla_gate_bias, gla_norm, w_branch_fox, w_branch_swa, w_branch_gla, w_out,
           ffn2_norm, ffn2_w_gu, ffn2_w_down, final_norm):
    batch, seq, d = x.shape
    depth = w_in.shape[0]
    xt = x.reshape(batch * seq, d)

    w_raw, w_act, w_small = _prep_w_in(w_in)
    w2p = jnp.zeros((depth, SMALL_WIDTH, gla_gate_w2.shape[-1]), BF16)
    w2p = w2p.at[:, FOX_HEADS:FOX_HEADS + GLA_RANK, :].set(gla_gate_w2.astype(BF16))
    fbias = jnp.zeros((depth, 1, SMALL_WIDTH), F32).at[:, 0, :FOX_HEADS].set(fox_forget_bias)
    f1gu, f1d = ffn1_w_gu.astype(BF16), ffn1_w_down.astype(BF16)
    f2gu, f2d = ffn2_w_gu.astype(BF16), ffn2_w_down.astype(BF16)
    wbf, wbs = w_branch_fox.astype(BF16), w_branch_swa.astype(BF16)
    wbg, wo = w_branch_gla.astype(BF16), w_out.astype(BF16)
    fw = final_norm.reshape(1, d)

    for l in range(depth):
        xt = _ffn(xt, ffn1_norm[l].reshape(1, d), f1gu, f1d, fw, layer=l, final_norm=False)
        raw, act, logf, bcum = _inproj(xt, mix_norm[l].reshape(1, d), w_raw, w_act, w_small, w2p,
                                       fbias[l], gla_gate_bias[l].reshape(1, -1), layer=l)
        o_fox = _fox(raw, logf, batch=batch, seq=seq)
        o_swa = _swa(raw, swa_sinks[l], batch=batch, seq=seq)
        o_gla = _gla(raw, bcum, act, gla_norm[l].reshape(1, -1), batch=batch, seq=seq)
        xt = _merge(xt, o_fox, o_swa, o_gla, act, wbf, wbs, wbg, wo, layer=l)
        xt = _ffn(xt, ffn2_norm[l].reshape(1, d), f2gu, f2d, fw, layer=l,
                  final_norm=(l == depth - 1))
    return xt.reshape(batch, seq, d)
```

```python
import functools

import jax
import jax.numpy as jnp
from jax import lax
from jax.experimental import pallas as pl
from jax.experimental.pallas import tpu as pltpu

F32 = jnp.float32
BF16 = jnp.bfloat16

EPS = 1e-6
NEG_INF = -1e30
HEAD_DIM = 64
FOX_HEADS = 8
SWA_Q_HEADS = 8
SWA_KV_HEADS = 2
WINDOW = 128
GLA_HEADS = 4
GLA_HEAD_K = 128
GLA_HEAD_V = 256
GLA_RANK = 16
GLA_TAU = 16.0
GLA_CHUNK = 64
GLA_SUPER = 256
FOX_Q_TILE = 1024
FOX_Q_SCALE = HEAD_DIM ** -0.5 * 1.4426950408889634
SWA_UNROLL = 3
LANES = 128
VMEM_LIMIT_BYTES = 56 * 1024 * 1024

RAW_FQ, RAW_FK, RAW_FV = 0, 512, 1024
RAW_SQ, RAW_SK, RAW_SV = 1536, 2048, 2304
RAW_GQ, RAW_GK, RAW_GV = 2560, 3072, 3584
RAW_WIDTH = 4608
ACT_WIDTH = 4096
SMALL_WIDTH = 128


def _params(n_grid_axes):
    return pltpu.CompilerParams(
        dimension_semantics=("arbitrary",) * n_grid_axes,
        vmem_limit_bytes=VMEM_LIMIT_BYTES)


def _resident(shape):
    nd = len(shape)
    return pl.BlockSpec(shape, lambda *_: (0,) * nd, pipeline_mode=pl.Buffered(1))


def _layer(shape, layer):
    nd = len(shape)
    return pl.BlockSpec((None,) + tuple(shape), lambda *_: (layer,) + (0,) * nd,
                        pipeline_mode=pl.Buffered(1))


def _rmsnorm(x, w):
    return x * lax.rsqrt(jnp.mean(x * x, axis=-1, keepdims=True) + EPS) * w


def _log_sigmoid(x):
    return jnp.minimum(x, 0.0) - jnp.log1p(jnp.exp(-jnp.abs(x)))


def _split3(x):
    p1 = x.astype(BF16)
    r1 = x - p1.astype(F32)
    p2 = r1.astype(BF16)
    p3 = (r1 - p2.astype(F32)).astype(BF16)
    return p1, p2, p3


def _lane_repeat(x, n):
    return jnp.concatenate([x] * n, axis=1)


def _dot(a, b):
    return jnp.dot(a, b, preferred_element_type=F32)


def _dot_nt(a, b):
    return lax.dot_general(a, b, (((1,), (1,)), ((), ())), preferred_element_type=F32)


def _dot_tn(a, b):
    return lax.dot_general(a, b, (((0,), (0,)), ((), ())), preferred_element_type=F32)


def _exact_tri_cumsum(tri, x):
    p1, p2, p3 = _split3(x)
    return _dot(tri, p1) + _dot(tri, p2) + _dot(tri, p3)


def _ffn_kernel(x_ref, nw_ref, wgu_ref, wd_ref, fw_ref, o_ref, h_ref, a_ref, *,
                f_chunk, final_norm):
    h_ref[...] = _rmsnorm(x_ref[...], nw_ref[...]).astype(BF16)
    d_ff = wd_ref.shape[0]
    for c in range(d_ff // f_chunk):
        sl = slice(c * f_chunk, (c + 1) * f_chunk)
        g = _dot(h_ref[...], wgu_ref[:, sl])
        u = _dot(h_ref[...], wgu_ref[:, d_ff + c * f_chunk:d_ff + (c + 1) * f_chunk])
        a_ref[:, sl] = (g * jax.nn.sigmoid(g) * u).astype(BF16)
    y = x_ref[...] + 0.5 * _dot(a_ref[...], wd_ref[...])
    if final_norm:
        y = _rmsnorm(y, fw_ref[...])
    o_ref[...] = y


def _ffn(x, nw, wgu, wd, fw, *, layer, final_norm, tm=1024, f_chunk=256):
    t, d = x.shape
    d_ff = wd.shape[1]
    row = pl.BlockSpec((tm, d), lambda i: (i, 0))
    return pl.pallas_call(
        functools.partial(_ffn_kernel, f_chunk=f_chunk, final_norm=final_norm),
        grid=(t // tm,),
        in_specs=[row, _resident((1, d)), _layer((d, 2 * d_ff), layer),
                  _layer((d_ff, d), layer), _resident((1, d))],
        out_specs=row,
        out_shape=jax.ShapeDtypeStruct((t, d), F32),
        scratch_shapes=[pltpu.VMEM((tm, d), BF16), pltpu.VMEM((tm, d_ff), BF16)],
        compiler_params=_params(1),
        name="ffn",
    )(x, nw, wgu, wd, fw)


def _inproj_kernel(x_ref, nw_ref, wraw_ref, wact_ref, wsm_ref, w2_ref, fb_ref, gb_ref,
                   raw_ref, act_ref, logf_ref, bcum_ref, h_ref, *, n_chunk, gr_width):
    tm = x_ref.shape[0]
    h_ref[...] = _rmsnorm(x_ref[...], nw_ref[...]).astype(BF16)

    def stage_small(_):
        small = _dot(h_ref[...], wsm_ref[...])
        logf_ref[...] = _log_sigmoid(small + fb_ref[...])
        return small.astype(BF16)

    def stage_gate(small):
        gate_logit = _dot(small, w2_ref[...]) + gb_ref[...]
        return _split3(_log_sigmoid(gate_logit) * (1.0 / GLA_TAU))

    def stage_cumsum(parts):
        r = lax.broadcasted_iota(jnp.int32, (GLA_SUPER, GLA_SUPER), 0)
        c = lax.broadcasted_iota(jnp.int32, (GLA_SUPER, GLA_SUPER), 1)
        tri = jnp.where((r >= c) & ((r // GLA_CHUNK) == (c // GLA_CHUNK)), 1.0, 0.0).astype(BF16)
        for s in range(tm // GLA_SUPER):
            rows = slice(s * GLA_SUPER, (s + 1) * GLA_SUPER)
            bcum_ref[rows, :] = sum(_dot(tri, p[rows, :]) for p in parts)

    stages = {0: stage_small, 3: stage_gate, 7: stage_cumsum}
    carried = None
    for c in range(raw_ref.shape[1] // n_chunk):
        if c in stages:
            carried = stages[c](carried)
        sl = slice(c * n_chunk, (c + 1) * n_chunk)
        z = _dot(h_ref[...], wraw_ref[:, sl])
        if sl.stop <= RAW_FK:
            z = z * FOX_Q_SCALE
        raw_ref[:, sl] = z.astype(BF16)
    for c in range(act_ref.shape[1] // n_chunk):
        sl = slice(c * n_chunk, (c + 1) * n_chunk)
        z = _dot(h_ref[...], wact_ref[:, sl])
        if c * n_chunk < gr_width:
            act_ref[:, sl] = (z * jax.nn.sigmoid(z)).astype(BF16)
        else:
            act_ref[:, sl] = jax.nn.sigmoid(z).astype(BF16)


def _inproj(x, nw, w_raw, w_act, w_small, w2p, fbias, gbias, *, layer, tm=512, n_chunk=512):
    t, d = x.shape
    gk = w2p.shape[2]
    row = lambda w: pl.BlockSpec((tm, w), lambda i: (i, 0))
    return pl.pallas_call(
        functools.partial(_inproj_kernel, n_chunk=n_chunk, gr_width=GLA_HEADS * GLA_HEAD_V),
        grid=(t // tm,),
        in_specs=[row(d), _resident((1, d)), _layer((d, RAW_WIDTH), layer),
                  _layer((d, ACT_WIDTH), layer), _layer((d, SMALL_WIDTH), layer),
                  _layer((SMALL_WIDTH, gk), layer),
                  _resident((1, SMALL_WIDTH)), _resident((1, gk))],
        out_specs=[row(RAW_WIDTH), row(ACT_WIDTH), row(SMALL_WIDTH), row(gk)],
        out_shape=[jax.ShapeDtypeStruct((t, RAW_WIDTH), BF16),
                   jax.ShapeDtypeStruct((t, ACT_WIDTH), BF16),
                   jax.ShapeDtypeStruct((t, SMALL_WIDTH), F32),
                   jax.ShapeDtypeStruct((t, gk), F32)],
        scratch_shapes=[pltpu.VMEM((tm, d), BF16)],
        compiler_params=_params(1),
        name="inproj",
    )(x, nw, w_raw, w_act, w_small, w2p, fbias, gbias)


def _fox_kernel(q_ref, k_ref, v_ref, logf_ref, o_ref,
                p_ref, qa_ref, qb_ref, ka_ref, kb_ref, va_ref, vb_ref, m_ref, acc_ref, *, tq):
    s_len = q_ref.shape[0]
    pair = pl.program_id(1)
    blk = LANES
    log2e = 1.4426950408889634
    n_heads = FOX_HEADS
    one_lane = 3 * n_heads
    lane = lax.broadcasted_iota(jnp.int32, (1, LANES), 1)

    @pl.when(pair == 0)
    def _():
        r = lax.broadcasted_iota(jnp.int32, (blk, blk), 0)
        cc = lax.broadcasted_iota(jnp.int32, (blk, blk), 1)
        tri = jnp.where(r >= cc, 1.0, 0.0).astype(BF16)
        ones = jnp.where(lane == one_lane, 1.0, 0.0)
        carry = jnp.zeros((1, LANES), F32)
        for n in range(s_len // blk):
            rows = slice(n * blk, (n + 1) * blk)
            within = _exact_tri_cumsum(tri, logf_ref[rows, :]) + carry
            carry = within[blk - 1:blk, :]
            p1, p2, p3 = (p.astype(F32) for p in _split3(jnp.where(lane < n_heads,
                                                                    within * log2e, 0.0)))
            packed = (p1 + pltpu.roll(p2, n_heads, axis=1) + pltpu.roll(p3, 2 * n_heads, axis=1)
                      + ones)
            p_ref[rows, :] = packed.astype(BF16)

    row = lax.broadcasted_iota(jnp.int32, (LANES, 2 * LANES), 0)
    col = lax.broadcasted_iota(jnp.int32, (LANES, 2 * LANES), 1)
    k_side = col >= LANES
    dst = col % LANES
    aug = dst % HEAD_DIM
    head = 2 * pair + jnp.where(dst < HEAD_DIM, 1, 0)
    part = aug - jnp.where(k_side, 3, 0)
    from_c = (part >= 0) & (part < 3) & (row == head + n_heads * part)
    ones_pos = aug - jnp.where(k_side, 0, 3)
    from_one = (row == one_lane) & (ones_pos >= 0) & (ones_pos < 3)
    sel = jnp.where(from_c, jnp.where(k_side, -1.0, 1.0), jnp.where(from_one, 1.0, 0.0))
    spare = _dot(p_ref[...], sel.astype(BF16))
    q_spare = spare[:, :LANES].astype(BF16)
    k_spare = spare[:, LANES:].astype(BF16)

    low = lax.broadcasted_iota(jnp.int32, (s_len, LANES), 1) < HEAD_DIM
    q, k, v = q_ref[...], k_ref[...], v_ref[...]
    qa_ref[...] = jnp.where(low, q, q_spare)
    qb_ref[...] = jnp.where(low, q_spare, q)
    ka_ref[...] = jnp.where(low, k, k_spare)
    kb_ref[...] = jnp.where(low, k_spare, k)
    va_ref[...] = jnp.where(low, v, jnp.where(lane == HEAD_DIM, 1.0, 0.0).astype(BF16))
    vb_ref[...] = jnp.where(low, jnp.where(lane == 0, 1.0, 0.0).astype(BF16), v)

    heads = ((qa_ref, ka_ref, va_ref), (qb_ref, kb_ref, vb_ref))

    def scores(h, q0, nq, k_rows, bias):
        q_ref_h, k_ref_h, _ = heads[h]
        s = _dot_nt(q_ref_h[q0:q0 + nq, :], k_ref_h[k_rows, :])
        return s if bias is None else s + bias

    def start(h, q0, nq, k_rows, bias):
        s = scores(h, q0, nq, k_rows, bias)
        m = jnp.max(s, axis=-1, keepdims=True) + jnp.zeros((nq, LANES), F32)
        p = jnp.exp2(s - _lane_repeat(m, s.shape[1] // LANES))
        acc_ref[h, q0:q0 + nq, :] = _dot(p.astype(BF16), heads[h][2][k_rows, :])
        m_ref[h, q0:q0 + nq, :] = m

    def update(h, q0, nq, k_rows):
        s = scores(h, q0, nq, k_rows, None)
        m_prev = m_ref[h, q0:q0 + nq, :]
        m_new = jnp.maximum(m_prev, jnp.max(s, axis=-1, keepdims=True))
        alpha = jnp.exp2(m_prev - m_new)
        p = jnp.exp2(s - _lane_repeat(m_new, s.shape[1] // LANES))
        acc_ref[h, q0:q0 + nq, :] = (alpha * acc_ref[h, q0:q0 + nq, :]
                                     + _dot(p.astype(BF16), heads[h][2][k_rows, :]))
        m_ref[h, q0:q0 + nq, :] = m_new

    half = tq // 2
    ri = lax.broadcasted_iota(jnp.int32, (half, tq), 0)
    ci = lax.broadcasted_iota(jnp.int32, (half, tq), 1)
    upper_bias = jnp.where(lax.broadcasted_iota(jnp.int32, (half, half), 1)
                           <= lax.broadcasted_iota(jnp.int32, (half, half), 0), 0.0, NEG_INF)
    lower_bias = jnp.where(ci <= ri + half, 0.0, NEG_INF)
    n_tiles = s_len // tq
    for qi in range(n_tiles):
        q0 = qi * tq
        for h in range(2):
            start(h, q0, half, slice(q0, q0 + half), upper_bias)
            start(h, q0 + half, half, slice(q0, q0 + tq), lower_bias)

    for qi in range(1, n_tiles):
        def kv_step(kj, _, q0=qi * tq):
            k_rows = pl.ds(pl.multiple_of(kj * tq, tq), tq)
            for h in range(2):
                update(h, q0, tq, k_rows)
            return 0

        lax.fori_loop(0, qi, kv_step, 0)

    acc_a, acc_b = acc_ref[0], acc_ref[1]
    l_a = jnp.sum(jnp.where(lane == HEAD_DIM, acc_a, 0.0), axis=-1, keepdims=True)
    l_b = jnp.sum(jnp.where(lane == 0, acc_b, 0.0), axis=-1, keepdims=True)
    o_ref[...] = jnp.where(lane < HEAD_DIM, acc_a / l_a, acc_b / l_b).astype(BF16)


def _fox(raw, logf, *, batch, seq):
    t = raw.shape[0]
    tq = min(FOX_Q_TILE, seq)
    n_pairs = FOX_HEADS // 2
    blk = lambda off: pl.BlockSpec((seq, LANES), lambda b, j: (b, off // LANES + j))
    return pl.pallas_call(
        functools.partial(_fox_kernel, tq=tq),
        grid=(batch, n_pairs),
        in_specs=[blk(RAW_FQ), blk(RAW_FK), blk(RAW_FV),
                  pl.BlockSpec((seq, LANES), lambda b, j: (b, 0))],
        out_specs=pl.BlockSpec((seq, LANES), lambda b, j: (b, j)),
        out_shape=jax.ShapeDtypeStruct((t, FOX_HEADS * HEAD_DIM), BF16),
        scratch_shapes=[pltpu.VMEM((seq, LANES), BF16)] * 7
                       + [pltpu.VMEM((2, seq, LANES), F32)] * 2,
        compiler_params=_params(2),
        name="fox",
    )(raw, raw, raw, logf)


def _swa_kernel(sink_ref, q_ref, k_ref, v_ref, o_ref):
    s_len = q_ref.shape[0]
    w = WINDOW
    group = SWA_Q_HEADS // SWA_KV_HEADS
    log2e = 1.4426950408889634
    scale = HEAD_DIM ** -0.5 * log2e
    low = lax.broadcasted_iota(jnp.int32, (w, LANES), 1) < HEAD_DIM
    i = lax.broadcasted_iota(jnp.int32, (w, 2 * w), 0)
    j = lax.broadcasted_iota(jnp.int32, (w, 2 * w), 1)
    band = jnp.where((j > i) & (j <= i + w), 0.0, NEG_INF)
    band = jnp.concatenate([band] * group, axis=0)
    it = lax.broadcasted_iota(jnp.int32, (w, w), 0)
    jt = lax.broadcasted_iota(jnp.int32, (w, w), 1)
    tri = jnp.concatenate([jnp.where(jt <= it, 0.0, NEG_INF)] * group, axis=0)
    hrow = lax.broadcasted_iota(jnp.int32, (group * w, LANES), 0) // w

    def block(n, kvh, sink, first):
        q_rows = pl.ds(pl.multiple_of(n * w, w), w)
        qs = []
        for p in range(group // 2):
            qp = q_ref[q_rows, (kvh * group // 2 + p) * LANES:(kvh * group // 2 + p + 1) * LANES]
            zero = jnp.zeros_like(qp)
            qs += [jnp.where(low, qp, zero), jnp.where(low, zero, qp)]
        q = jnp.concatenate(qs, axis=0)
        k_rows = q_rows if first else pl.ds(pl.multiple_of((n - 1) * w, w), 2 * w)
        kv_cols = slice(kvh * LANES, (kvh + 1) * LANES)
        kk = k_ref[k_rows, kv_cols]
        vv = v_ref[k_rows, kv_cols]
        s = _dot_nt(q, kk) * scale + (tri if first else band)
        m = jnp.maximum(jnp.max(s, axis=-1, keepdims=True), sink)
        p = jnp.exp2(s - (m if first else _lane_repeat(m, 2 * w // LANES)))
        denom = jnp.sum(p, axis=-1, keepdims=True) + jnp.exp2(sink - m)
        o = _dot(p.astype(BF16), vv) / denom
        for pidx in range(group // 2):
            oa = o[(2 * pidx) * w:(2 * pidx + 1) * w, :]
            ob = o[(2 * pidx + 1) * w:(2 * pidx + 2) * w, :]
            col = (kvh * group // 2 + pidx) * LANES
            o_ref[q_rows, col:col + LANES] = jnp.where(low, oa, ob).astype(BF16)

    sinks = []
    for kvh in range(SWA_KV_HEADS):
        sink = jnp.zeros((group * w, LANES), F32)
        for g in range(group):
            sink = jnp.where(hrow == g, sink_ref[kvh * group + g] * log2e, sink)
        sinks.append(sink)
        block(0, kvh, sink, True)

    def body(n, _):
        for kvh in range(SWA_KV_HEADS):
            block(n, kvh, sinks[kvh], False)
        return 0

    lax.fori_loop(1, s_len // w, body, 0, unroll=SWA_UNROLL)


def _swa(raw, sinks, *, batch, seq):
    t = raw.shape[0]
    qw = SWA_Q_HEADS * HEAD_DIM
    kw = SWA_KV_HEADS * LANES
    return pl.pallas_call(
        _swa_kernel,
        grid=(batch,),
        in_specs=[pl.BlockSpec(memory_space=pltpu.SMEM),
                  pl.BlockSpec((seq, qw), lambda b: (b, RAW_SQ // qw)),
                  pl.BlockSpec((seq, kw), lambda b: (b, RAW_SK // kw)),
                  pl.BlockSpec((seq, kw), lambda b: (b, RAW_SV // kw))],
        out_specs=pl.BlockSpec((seq, qw), lambda b: (b, 0)),
        out_shape=jax.ShapeDtypeStruct((t, qw), BF16),
        compiler_params=_params(1),
        name="swa",
    )(sinks, raw, raw, raw)


def _gla_kernel(q_ref, k_ref, v_ref, b_ref, gr_ref, nw_ref, o_ref, st_ref, *, n_heads, unroll):
    s_len = q_ref.shape[0]
    ck, sup = GLA_CHUNK, GLA_SUPER
    dk, dv = GLA_HEAD_K, GLA_HEAD_V
    per = sup // ck
    scale = dk ** -0.5
    r = lax.broadcasted_iota(jnp.int32, (sup, sup), 0)
    c = lax.broadcasted_iota(jnp.int32, (sup, sup), 1)
    intra = (r >= c) & ((r // ck) == (c // ck))
    st_ref[...] = jnp.zeros(st_ref.shape, F32)

    def head_block(rows, h):
        kcols = slice(h * dk, (h + 1) * dk)
        vcols = slice(h * dv, (h + 1) * dv)
        b = b_ref[rows, kcols]
        b3 = b.reshape(per, ck, dk)
        b_last = b3[:, ck - 1:ck, :]
        qf = q_ref[rows, kcols].astype(F32) * scale
        kf = k_ref[rows, kcols].astype(F32)
        q_t = (qf * jnp.exp(b)).astype(BF16)
        k_t = (kf * jnp.exp(-b)).astype(BF16)
        k_s = (kf.reshape(per, ck, dk) * jnp.exp(b_last - b3)).astype(BF16)
        decay = jnp.exp(b_last)
        v = v_ref[rows, vcols]
        a = jnp.where(intra, _dot_nt(q_t, k_t), 0.0)
        o_intra = _dot(a.astype(BF16), v)
        st = st_ref[h]
        outs = []
        for cidx in range(per):
            chunk = slice(cidx * ck, (cidx + 1) * ck)
            outs.append(o_intra[chunk, :] + _dot_nt(q_t[chunk, :], st.astype(BF16)))
            st = st * decay[cidx] + _dot_tn(v[chunk, :], k_s[cidx])
        st_ref[h] = st
        o = jnp.concatenate(outs, axis=0)
        o = o * lax.rsqrt(jnp.mean(o * o, axis=-1, keepdims=True) + EPS) * nw_ref[...]
        o_ref[rows, vcols] = (o * gr_ref[rows, vcols].astype(F32)).astype(BF16)

    def super_block(sb, _):
        rows = pl.ds(pl.multiple_of(sb * sup, sup), sup)
        for h in range(n_heads):
            head_block(rows, h)
        return 0

    lax.fori_loop(0, s_len // sup, super_block, 0, unroll=unroll)


def _gla(raw, bcum, act, gla_norm, *, batch, seq, heads_per_step=2, unroll=2):
    t = raw.shape[0]
    dk, dv = GLA_HEAD_K * heads_per_step, GLA_HEAD_V * heads_per_step
    assert RAW_GQ % dk == 0 and RAW_GK % dk == 0 and RAW_GV % dv == 0
    return pl.pallas_call(
        functools.partial(_gla_kernel, n_heads=heads_per_step, unroll=unroll),
        grid=(batch, GLA_HEADS // heads_per_step),
        in_specs=[pl.BlockSpec((seq, dk), lambda b, h: (b, RAW_GQ // dk + h)),
                  pl.BlockSpec((seq, dk), lambda b, h: (b, RAW_GK // dk + h)),
                  pl.BlockSpec((seq, dv), lambda b, h: (b, RAW_GV // dv + h)),
                  pl.BlockSpec((seq, dk), lambda b, h: (b, h)),
                  pl.BlockSpec((seq, dv), lambda b, h: (b, h)),
                  pl.BlockSpec((1, GLA_HEAD_V), lambda b, h: (0, 0))],
        out_specs=pl.BlockSpec((seq, dv), lambda b, h: (b, h)),
        out_shape=jax.ShapeDtypeStruct((t, GLA_HEADS * GLA_HEAD_V), BF16),
        scratch_shapes=[pltpu.VMEM((heads_per_step, GLA_HEAD_V, GLA_HEAD_K), F32)],
        compiler_params=_params(2),
        name="gla",
    )(raw, raw, raw, bcum, act, gla_norm)


def _merge_kernel(x_ref, of_ref, os_ref, og_ref, g0_ref, g1_ref, g2_ref,
                  wf_ref, ws_ref, wg_ref, wo_ref, o_ref):
    merged = (g0_ref[...].astype(F32) * _dot(of_ref[...], wf_ref[...])
              + g1_ref[...].astype(F32) * _dot(os_ref[...], ws_ref[...])
              + g2_ref[...].astype(F32) * _dot(og_ref[...], wg_ref[...]))
    o_ref[...] = x_ref[...] + _dot(merged.astype(BF16), wo_ref[...])


def _merge(x, o_fox, o_swa, o_gla, act, wf, ws, wg, wo, *, layer, tm=1024):
    t, d = x.shape
    row = lambda w, j=0: pl.BlockSpec((tm, w), lambda i: (i, j))
    return pl.pallas_call(
        _merge_kernel,
        grid=(t // tm,),
        in_specs=[row(d), row(o_fox.shape[1]), row(o_swa.shape[1]), row(o_gla.shape[1]),
                  row(d, 1), row(d, 2), row(d, 3),
                  _layer(wf.shape[1:], layer), _layer(ws.shape[1:], layer),
                  _layer(wg.shape[1:], layer), _layer(wo.shape[1:], layer)],
        out_specs=row(d),
        out_shape=jax.ShapeDtypeStruct((t, d), F32),
        compiler_params=_params(1),
        name="merge",
    )(x, o_fox, o_swa, o_gla, act, act, act, wf, ws, wg, wo)


def _w_in_pieces():
    sizes = (512, 512, 512, FOX_HEADS, 512, 128, 128, 512, 512, 1024, GLA_RANK, 1024, 3072)
    offs = [0]
    for s in sizes:
        offs.append(offs[-1] + s)
    fq, fk, fv, ff, sq, sk, sv, gq, gk, gv, glr, gr, gates = offs[:-1]
    h = HEAD_DIM
    raw = [(fq, 512), (fk, 512), (fv, 512), (sq, 512),
           (sk, h), (sk, h), (sk + h, h), (sk + h, h),
           (sv, h), (sv, h), (sv + h, h), (sv + h, h),
           (gq, 512), (gk, 512), (gv, 1024)]
    moves, dst = [], 0
    for src, width in raw:
        moves.append(("raw", dst, src, width))
        dst += width
    assert dst == RAW_WIDTH
    moves.append(("act", 0, gr, ACT_WIDTH))
    moves += [("small", 0, ff, FOX_HEADS), ("small", FOX_HEADS, glr, GLA_RANK)]
    return moves


def _w_in_kernel(w_ref, raw_ref, act_ref, small_ref):
    outs = {"raw": raw_ref, "act": act_ref, "small": small_ref}
    small_ref[...] = jnp.zeros(small_ref.shape, BF16)
    for name, dst, src, width in _w_in_pieces():
        outs[name][:, dst:dst + width] = w_ref[:, src:src + width].astype(BF16)


def _prep_w_in(w_in, *, rows=256):
    depth, d, d_in = w_in.shape
    out = lambda w: pl.BlockSpec((None, rows, w), lambda l, i: (l, i, 0))
    return pl.pallas_call(
        _w_in_kernel,
        grid=(depth, d // rows),
        in_specs=[pl.BlockSpec((None, rows, d_in), lambda l, i: (l, i, 0))],
        out_specs=[out(RAW_WIDTH), out(ACT_WIDTH), out(SMALL_WIDTH)],
        out_shape=[jax.ShapeDtypeStruct((depth, d, RAW_WIDTH), BF16),
                   jax.ShapeDtypeStruct((depth, d, ACT_WIDTH), BF16),
                   jax.ShapeDtypeStruct((depth, d, SMALL_WIDTH), BF16)],
        compiler_params=_params(2),
        name="w_in_layout",
    )(w_in)


def kernel(x, ffn1_norm, ffn1_w_gu, ffn1_w_down, mix_norm, w_in, fox_forget_bias, swa_sinks,
           gla_gate_w2, gla_gate_bias, gla_norm, w_branch_fox, w_branch_swa, w_branch_gla, w_out,
           ffn2_norm, ffn2_w_gu, ffn2_w_down, final_norm):
    batch, seq, d = x.shape
    depth = w_in.shape[0]
    xt = x.reshape(batch * seq, d)

    w_raw, w_act, w_small = _prep_w_in(w_in)
    w2p = jnp.zeros((depth, SMALL_WIDTH, gla_gate_w2.shape[-1]), BF16)
    w2p = w2p.at[:, FOX_HEADS:FOX_HEADS + GLA_RANK, :].set(gla_gate_w2.astype(BF16))
    fbias = jnp.zeros((depth, 1, SMALL_WIDTH), F32).at[:, 0, :FOX_HEADS].set(fox_forget_bias)
    f1gu, f1d = ffn1_w_gu.astype(BF16), ffn1_w_down.astype(BF16)
    f2gu, f2d = ffn2_w_gu.astype(BF16), ffn2_w_down.astype(BF16)
    wbf, wbs = w_branch_fox.astype(BF16), w_branch_swa.astype(BF16)
    wbg, wo = w_branch_gla.astype(BF16), w_out.astype(BF16)
    fw = final_norm.reshape(1, d)

    for l in range(depth):
        xt = _ffn(xt, ffn1_norm[l].reshape(1, d), f1gu, f1d, fw, layer=l, final_norm=False)
        raw, act, logf, bcum = _inproj(xt, mix_norm[l].reshape(1, d), w_raw, w_act, w_small, w2p,
                                       fbias[l], gla_gate_bias[l].reshape(1, -1), layer=l)
        o_fox = _fox(raw, logf, batch=batch, seq=seq)
        o_swa = _swa(raw, swa_sinks[l], batch=batch, seq=seq)
        o_gla = _gla(raw, bcum, act, gla_norm[l].reshape(1, -1), batch=batch, seq=seq)
        xt = _merge(xt, o_fox, o_swa, o_gla, act, wbf, wbs, wbg, wo, layer=l)
        xt = _ffn(xt, ffn2_norm[l].reshape(1, d), f2gu, f2d, fw, layer=l,
                  final_norm=(l == depth - 1))
    return xt.reshape(batch, seq, d)
```

```python
import functools

import jax
import jax.numpy as jnp
from jax import lax
from jax.experimental import pallas as pl
from jax.experimental.pallas import tpu as pltpu

F32 = jnp.float32
BF16 = jnp.bfloat16

EPS = 1e-6
NEG_INF = -1e30
HEAD_DIM = 64
FOX_HEADS = 8
SWA_Q_HEADS = 8
SWA_KV_HEADS = 2
WINDOW = 128
GLA_HEADS = 4
GLA_HEAD_K = 128
GLA_HEAD_V = 256
GLA_RANK = 16
GLA_TAU = 16.0
GLA_CHUNK = 64
GLA_SUPER = 256
FOX_Q_TILE = 1024
FOX_Q_SCALE = HEAD_DIM ** -0.5 * 1.4426950408889634
SWA_UNROLL = 3
LANES = 128
VMEM_LIMIT_BYTES = 56 * 1024 * 1024

RAW_FQ, RAW_FK, RAW_FV = 0, 512, 1024
RAW_SQ, RAW_SK, RAW_SV = 1536, 2048, 2304
RAW_GQ, RAW_GK, RAW_GV = 2560, 3072, 3584
RAW_WIDTH = 4608
ACT_WIDTH = 4096
SMALL_WIDTH = 128


def _params(n_grid_axes):
    return pltpu.CompilerParams(
        dimension_semantics=("arbitrary",) * n_grid_axes,
        vmem_limit_bytes=VMEM_LIMIT_BYTES)


def _resident(shape):
    nd = len(shape)
    return pl.BlockSpec(shape, lambda *_: (0,) * nd, pipeline_mode=pl.Buffered(1))


def _layer(shape, layer):
    nd = len(shape)
    return pl.BlockSpec((None,) + tuple(shape), lambda *_: (layer,) + (0,) * nd,
                        pipeline_mode=pl.Buffered(1))


def _rmsnorm(x, w):
    return x * lax.rsqrt(jnp.mean(x * x, axis=-1, keepdims=True) + EPS) * w


def _log_sigmoid(x):
    return jnp.minimum(x, 0.0) - jnp.log1p(jnp.exp(-jnp.abs(x)))


def _split3(x):
    p1 = x.astype(BF16)
    r1 = x - p1.astype(F32)
    p2 = r1.astype(BF16)
    p3 = (r1 - p2.astype(F32)).astype(BF16)
    return p1, p2, p3


def _lane_repeat(x, n):
    return jnp.concatenate([x] * n, axis=1)


def _dot(a, b):
    return jnp.dot(a, b, preferred_element_type=F32)


def _dot_nt(a, b):
    return lax.dot_general(a, b, (((1,), (1,)), ((), ())), preferred_element_type=F32)


def _dot_tn(a, b):
    return lax.dot_general(a, b, (((0,), (0,)), ((), ())), preferred_element_type=F32)


def _exact_tri_cumsum(tri, x):
    p1, p2, p3 = _split3(x)
    return _dot(tri, p1) + _dot(tri, p2) + _dot(tri, p3)


def _ffn_kernel(x_ref, nw_ref, wgu_ref, wd_ref, fw_ref, o_ref, h_ref, a_ref, *,
                f_chunk, final_norm):
    h_ref[...] = _rmsnorm(x_ref[...], nw_ref[...]).astype(BF16)
    d_ff = wd_ref.shape[0]
    for c in range(d_ff // f_chunk):
        sl = slice(c * f_chunk, (c + 1) * f_chunk)
        g = _dot(h_ref[...], wgu_ref[:, sl])
        u = _dot(h_ref[...], wgu_ref[:, d_ff + c * f_chunk:d_ff + (c + 1) * f_chunk])
        a_ref[:, sl] = (g * jax.nn.sigmoid(g) * u).astype(BF16)
    y = x_ref[...] + 0.5 * _dot(a_ref[...], wd_ref[...])
    if final_norm:
        y = _rmsnorm(y, fw_ref[...])
    o_ref[...] = y


def _ffn(x, nw, wgu, wd, fw, *, layer, final_norm, tm=1024, f_chunk=256):
    t, d = x.shape
    d_ff = wd.shape[1]
    row = pl.BlockSpec((tm, d), lambda i: (i, 0))
    return pl.pallas_call(
        functools.partial(_ffn_kernel, f_chunk=f_chunk, final_norm=final_norm),
        grid=(t // tm,),
        in_specs=[row, _resident((1, d)), _layer((d, 2 * d_ff), layer),
                  _layer((d_ff, d), layer), _resident((1, d))],
        out_specs=row,
        out_shape=jax.ShapeDtypeStruct((t, d), F32),
        scratch_shapes=[pltpu.VMEM((tm, d), BF16), pltpu.VMEM((tm, d_ff), BF16)],
        compiler_params=_params(1),
        name="ffn",
    )(x, nw, wgu, wd, fw)


def _inproj_kernel(x_ref, nw_ref, wraw_ref, wact_ref, wsm_ref, w2_ref, fb_ref, gb_ref,
                   raw_ref, act_ref, logf_ref, bcum_ref, h_ref, *, n_chunk, gr_width):
    tm = x_ref.shape[0]
    h_ref[...] = _rmsnorm(x_ref[...], nw_ref[...]).astype(BF16)
    for c in range(raw_ref.shape[1] // n_chunk):
        sl = slice(c * n_chunk, (c + 1) * n_chunk)
        z = _dot_nt(h_ref[...], wraw_ref[sl, :])
        if sl.stop <= RAW_FK:
            z = z * FOX_Q_SCALE
        raw_ref[:, sl] = z.astype(BF16)
    for c in range(act_ref.shape[1] // n_chunk):
        sl = slice(c * n_chunk, (c + 1) * n_chunk)
        z = _dot_nt(h_ref[...], wact_ref[sl, :])
        if c * n_chunk < gr_width:
            act_ref[:, sl] = (z * jax.nn.sigmoid(z)).astype(BF16)
        else:
            act_ref[:, sl] = jax.nn.sigmoid(z).astype(BF16)
    small = _dot_nt(h_ref[...], wsm_ref[...])
    logf_ref[...] = _log_sigmoid(small + fb_ref[...])
    gate_logit = _dot(small.astype(BF16), w2_ref[...]) + gb_ref[...]
    log_a = _log_sigmoid(gate_logit) * (1.0 / GLA_TAU)
    r = lax.broadcasted_iota(jnp.int32, (GLA_SUPER, GLA_SUPER), 0)
    c = lax.broadcasted_iota(jnp.int32, (GLA_SUPER, GLA_SUPER), 1)
    tri = jnp.where((r >= c) & ((r // GLA_CHUNK) == (c // GLA_CHUNK)), 1.0, 0.0).astype(BF16)
    for s in range(tm // GLA_SUPER):
        rows = slice(s * GLA_SUPER, (s + 1) * GLA_SUPER)
        bcum_ref[rows, :] = _exact_tri_cumsum(tri, log_a[rows, :])


def _inproj(x, nw, w_raw, w_act, w_small, w2p, fbias, gbias, *, layer, tm=512, n_chunk=512):
    t, d = x.shape
    gk = w2p.shape[2]
    row = lambda w: pl.BlockSpec((tm, w), lambda i: (i, 0))
    return pl.pallas_call(
        functools.partial(_inproj_kernel, n_chunk=n_chunk, gr_width=GLA_HEADS * GLA_HEAD_V),
        grid=(t // tm,),
        in_specs=[row(d), _resident((1, d)), _layer((RAW_WIDTH, d), layer),
                  _layer((ACT_WIDTH, d), layer), _layer((SMALL_WIDTH, d), layer),
                  _layer((SMALL_WIDTH, gk), layer),
                  _resident((1, SMALL_WIDTH)), _resident((1, gk))],
        out_specs=[row(RAW_WIDTH), row(ACT_WIDTH), row(SMALL_WIDTH), row(gk)],
        out_shape=[jax.ShapeDtypeStruct((t, RAW_WIDTH), BF16),
                   jax.ShapeDtypeStruct((t, ACT_WIDTH), BF16),
                   jax.ShapeDtypeStruct((t, SMALL_WIDTH), F32),
                   jax.ShapeDtypeStruct((t, gk), F32)],
        scratch_shapes=[pltpu.VMEM((tm, d), BF16)],
        compiler_params=_params(1),
        name="inproj",
    )(x, nw, w_raw, w_act, w_small, w2p, fbias, gbias)


def _fox_kernel(q_ref, k_ref, v_ref, logf_ref, o_ref,
                p_ref, qa_ref, qb_ref, ka_ref, kb_ref, va_ref, vb_ref, m_ref, acc_ref, *, tq):
    s_len = q_ref.shape[0]
    pair = pl.program_id(1)
    blk = LANES
    log2e = 1.4426950408889634
    n_heads = FOX_HEADS
    one_lane = 3 * n_heads
    lane = lax.broadcasted_iota(jnp.int32, (1, LANES), 1)

    @pl.when(pair == 0)
    def _():
        r = lax.broadcasted_iota(jnp.int32, (blk, blk), 0)
        cc = lax.broadcasted_iota(jnp.int32, (blk, blk), 1)
        tri = jnp.where(r >= cc, 1.0, 0.0).astype(BF16)
        ones = jnp.where(lane == one_lane, 1.0, 0.0)
        carry = jnp.zeros((1, LANES), F32)
        for n in range(s_len // blk):
            rows = slice(n * blk, (n + 1) * blk)
            within = _exact_tri_cumsum(tri, logf_ref[rows, :]) + carry
            carry = within[blk - 1:blk, :]
            p1, p2, p3 = (p.astype(F32) for p in _split3(jnp.where(lane < n_heads,
                                                                    within * log2e, 0.0)))
            packed = (p1 + pltpu.roll(p2, n_heads, axis=1) + pltpu.roll(p3, 2 * n_heads, axis=1)
                      + ones)
            p_ref[rows, :] = packed.astype(BF16)

    row = lax.broadcasted_iota(jnp.int32, (LANES, 2 * LANES), 0)
    col = lax.broadcasted_iota(jnp.int32, (LANES, 2 * LANES), 1)
    k_side = col >= LANES
    dst = col % LANES
    aug = dst % HEAD_DIM
    head = 2 * pair + jnp.where(dst < HEAD_DIM, 1, 0)
    part = aug - jnp.where(k_side, 3, 0)
    from_c = (part >= 0) & (part < 3) & (row == head + n_heads * part)
    ones_pos = aug - jnp.where(k_side, 0, 3)
    from_one = (row == one_lane) & (ones_pos >= 0) & (ones_pos < 3)
    sel = jnp.where(from_c, jnp.where(k_side, -1.0, 1.0), jnp.where(from_one, 1.0, 0.0))
    spare = _dot(p_ref[...], sel.astype(BF16))
    q_spare = spare[:, :LANES].astype(BF16)
    k_spare = spare[:, LANES:].astype(BF16)

    low = lax.broadcasted_iota(jnp.int32, (s_len, LANES), 1) < HEAD_DIM
    q, k, v = q_ref[...], k_ref[...], v_ref[...]
    qa_ref[...] = jnp.where(low, q, q_spare)
    qb_ref[...] = jnp.where(low, q_spare, q)
    ka_ref[...] = jnp.where(low, k, k_spare)
    kb_ref[...] = jnp.where(low, k_spare, k)
    va_ref[...] = jnp.where(low, v, jnp.where(lane == HEAD_DIM, 1.0, 0.0).astype(BF16))
    vb_ref[...] = jnp.where(low, jnp.where(lane == 0, 1.0, 0.0).astype(BF16), v)

    heads = ((qa_ref, ka_ref, va_ref), (qb_ref, kb_ref, vb_ref))

    def scores(h, q0, nq, k_rows, bias):
        q_ref_h, k_ref_h, _ = heads[h]
        s = _dot_nt(q_ref_h[q0:q0 + nq, :], k_ref_h[k_rows, :])
        return s if bias is None else s + bias

    def start(h, q0, nq, k_rows, bias):
        s = scores(h, q0, nq, k_rows, bias)
        m = jnp.max(s, axis=-1, keepdims=True) + jnp.zeros((nq, LANES), F32)
        p = jnp.exp2(s - _lane_repeat(m, s.shape[1] // LANES))
        acc_ref[h, q0:q0 + nq, :] = _dot(p.astype(BF16), heads[h][2][k_rows, :])
        m_ref[h, q0:q0 + nq, :] = m

    def update(h, q0, nq, k_rows):
        s = scores(h, q0, nq, k_rows, None)
        m_prev = m_ref[h, q0:q0 + nq, :]
        m_new = jnp.maximum(m_prev, jnp.max(s, axis=-1, keepdims=True))
        alpha = jnp.exp2(m_prev - m_new)
        p = jnp.exp2(s - _lane_repeat(m_new, s.shape[1] // LANES))
        acc_ref[h, q0:q0 + nq, :] = (alpha * acc_ref[h, q0:q0 + nq, :]
                                     + _dot(p.astype(BF16), heads[h][2][k_rows, :]))
        m_ref[h, q0:q0 + nq, :] = m_new

    half = tq // 2
    ri = lax.broadcasted_iota(jnp.int32, (half, tq), 0)
    ci = lax.broadcasted_iota(jnp.int32, (half, tq), 1)
    upper_bias = jnp.where(lax.broadcasted_iota(jnp.int32, (half, half), 1)
                           <= lax.broadcasted_iota(jnp.int32, (half, half), 0), 0.0, NEG_INF)
    lower_bias = jnp.where(ci <= ri + half, 0.0, NEG_INF)
    n_tiles = s_len // tq
    for qi in range(n_tiles):
        q0 = qi * tq
        for h in range(2):
            start(h, q0, half, slice(q0, q0 + half), upper_bias)
            start(h, q0 + half, half, slice(q0, q0 + tq), lower_bias)

    for qi in range(1, n_tiles):
        def kv_step(kj, _, q0=qi * tq):
            k_rows = pl.ds(pl.multiple_of(kj * tq, tq), tq)
            for h in range(2):
                update(h, q0, tq, k_rows)
            return 0

        lax.fori_loop(0, qi, kv_step, 0)

    acc_a, acc_b = acc_ref[0], acc_ref[1]
    l_a = jnp.sum(jnp.where(lane == HEAD_DIM, acc_a, 0.0), axis=-1, keepdims=True)
    l_b = jnp.sum(jnp.where(lane == 0, acc_b, 0.0), axis=-1, keepdims=True)
    o_ref[...] = jnp.where(lane < HEAD_DIM, acc_a / l_a, acc_b / l_b).astype(BF16)


def _fox(raw, logf, *, batch, seq):
    t = raw.shape[0]
    tq = min(FOX_Q_TILE, seq)
    n_pairs = FOX_HEADS // 2
    blk = lambda off: pl.BlockSpec((seq, LANES), lambda b, j: (b, off // LANES + j))
    return pl.pallas_call(
        functools.partial(_fox_kernel, tq=tq),
        grid=(batch, n_pairs),
        in_specs=[blk(RAW_FQ), blk(RAW_FK), blk(RAW_FV),
                  pl.BlockSpec((seq, LANES), lambda b, j: (b, 0))],
        out_specs=pl.BlockSpec((seq, LANES), lambda b, j: (b, j)),
        out_shape=jax.ShapeDtypeStruct((t, FOX_HEADS * HEAD_DIM), BF16),
        scratch_shapes=[pltpu.VMEM((seq, LANES), BF16)] * 7
                       + [pltpu.VMEM((2, seq, LANES), F32)] * 2,
        compiler_params=_params(2),
        name="fox",
    )(raw, raw, raw, logf)


def _swa_kernel(sink_ref, q_ref, k_ref, v_ref, o_ref):
    s_len = q_ref.shape[0]
    w = WINDOW
    group = SWA_Q_HEADS // SWA_KV_HEADS
    log2e = 1.4426950408889634
    scale = HEAD_DIM ** -0.5 * log2e
    low = lax.broadcasted_iota(jnp.int32, (w, LANES), 1) < HEAD_DIM
    i = lax.broadcasted_iota(jnp.int32, (w, 2 * w), 0)
    j = lax.broadcasted_iota(jnp.int32, (w, 2 * w), 1)
    band = jnp.where((j > i) & (j <= i + w), 0.0, NEG_INF)
    band = jnp.concatenate([band] * group, axis=0)
    it = lax.broadcasted_iota(jnp.int32, (w, w), 0)
    jt = lax.broadcasted_iota(jnp.int32, (w, w), 1)
    tri = jnp.concatenate([jnp.where(jt <= it, 0.0, NEG_INF)] * group, axis=0)
    hrow = lax.broadcasted_iota(jnp.int32, (group * w, LANES), 0) // w

    def block(n, kvh, sink, first):
        q_rows = pl.ds(pl.multiple_of(n * w, w), w)
        qs = []
        for p in range(group // 2):
            qp = q_ref[q_rows, (kvh * group // 2 + p) * LANES:(kvh * group // 2 + p + 1) * LANES]
            zero = jnp.zeros_like(qp)
            qs += [jnp.where(low, qp, zero), jnp.where(low, zero, qp)]
        q = jnp.concatenate(qs, axis=0)
        k_rows = q_rows if first else pl.ds(pl.multiple_of((n - 1) * w, w), 2 * w)
        kv_cols = slice(kvh * LANES, (kvh + 1) * LANES)
        kk = k_ref[k_rows, kv_cols]
        vv = v_ref[k_rows, kv_cols]
        s = _dot_nt(q, kk) * scale + (tri if first else band)
        m = jnp.maximum(jnp.max(s, axis=-1, keepdims=True), sink)
        p = jnp.exp2(s - (m if first else _lane_repeat(m, 2 * w // LANES)))
        denom = jnp.sum(p, axis=-1, keepdims=True) + jnp.exp2(sink - m)
        o = _dot(p.astype(BF16), vv) / denom
        for pidx in range(group // 2):
            oa = o[(2 * pidx) * w:(2 * pidx + 1) * w, :]
            ob = o[(2 * pidx + 1) * w:(2 * pidx + 2) * w, :]
            col = (kvh * group // 2 + pidx) * LANES
            o_ref[q_rows, col:col + LANES] = jnp.where(low, oa, ob).astype(BF16)

    sinks = []
    for kvh in range(SWA_KV_HEADS):
        sink = jnp.zeros((group * w, LANES), F32)
        for g in range(group):
            sink = jnp.where(hrow == g, sink_ref[kvh * group + g] * log2e, sink)
        sinks.append(sink)
        block(0, kvh, sink, True)

    def body(n, _):
        for kvh in range(SWA_KV_HEADS):
            block(n, kvh, sinks[kvh], False)
        return 0

    lax.fori_loop(1, s_len // w, body, 0, unroll=SWA_UNROLL)


def _swa(raw, sinks, *, batch, seq):
    t = raw.shape[0]
    qw = SWA_Q_HEADS * HEAD_DIM
    kw = SWA_KV_HEADS * LANES
    return pl.pallas_call(
        _swa_kernel,
        grid=(batch,),
        in_specs=[pl.BlockSpec(memory_space=pltpu.SMEM),
                  pl.BlockSpec((seq, qw), lambda b: (b, RAW_SQ // qw)),
                  pl.BlockSpec((seq, kw), lambda b: (b, RAW_SK // kw)),
                  pl.BlockSpec((seq, kw), lambda b: (b, RAW_SV // kw))],
        out_specs=pl.BlockSpec((seq, qw), lambda b: (b, 0)),
        out_shape=jax.ShapeDtypeStruct((t, qw), BF16),
        compiler_params=_params(1),
        name="swa",
    )(sinks, raw, raw, raw)


def _gla_kernel(q_ref, k_ref, v_ref, b_ref, gr_ref, nw_ref, o_ref, st_ref, *, n_heads, unroll):
    s_len = q_ref.shape[0]
    ck, sup = GLA_CHUNK, GLA_SUPER
    dk, dv = GLA_HEAD_K, GLA_HEAD_V
    per = sup // ck
    scale = dk ** -0.5
    r = lax.broadcasted_iota(jnp.int32, (sup, sup), 0)
    c = lax.broadcasted_iota(jnp.int32, (sup, sup), 1)
    intra = (r >= c) & ((r // ck) == (c // ck))
    st_ref[...] = jnp.zeros(st_ref.shape, F32)

    def head_block(rows, h):
        kcols = slice(h * dk, (h + 1) * dk)
        vcols = slice(h * dv, (h + 1) * dv)
        b = b_ref[rows, kcols]
        b3 = b.reshape(per, ck, dk)
        b_last = b3[:, ck - 1:ck, :]
        qf = q_ref[rows, kcols].astype(F32) * scale
        kf = k_ref[rows, kcols].astype(F32)
        q_t = (qf * jnp.exp(b)).astype(BF16)
        k_t = (kf * jnp.exp(-b)).astype(BF16)
        k_s = (kf.reshape(per, ck, dk) * jnp.exp(b_last - b3)).astype(BF16)
        decay = jnp.exp(b_last)
        v = v_ref[rows, vcols]
        a = jnp.where(intra, _dot_nt(q_t, k_t), 0.0)
        o_intra = _dot(a.astype(BF16), v)
        st = st_ref[h]
        outs = []
        for cidx in range(per):
            chunk = slice(cidx * ck, (cidx + 1) * ck)
            outs.append(o_intra[chunk, :] + _dot_nt(q_t[chunk, :], st.astype(BF16)))
            st = st * decay[cidx] + _dot_tn(v[chunk, :], k_s[cidx])
        st_ref[h] = st
        o = jnp.concatenate(outs, axis=0)
        o = o * lax.rsqrt(jnp.mean(o * o, axis=-1, keepdims=True) + EPS) * nw_ref[...]
        o_ref[rows, vcols] = (o * gr_ref[rows, vcols].astype(F32)).astype(BF16)

    def super_block(sb, _):
        rows = pl.ds(pl.multiple_of(sb * sup, sup), sup)
        for h in range(n_heads):
            head_block(rows, h)
        return 0

    lax.fori_loop(0, s_len // sup, super_block, 0, unroll=unroll)


def _gla(raw, bcum, act, gla_norm, *, batch, seq, heads_per_step=2, unroll=2):
    t = raw.shape[0]
    dk, dv = GLA_HEAD_K * heads_per_step, GLA_HEAD_V * heads_per_step
    assert RAW_GQ % dk == 0 and RAW_GK % dk == 0 and RAW_GV % dv == 0
    return pl.pallas_call(
        functools.partial(_gla_kernel, n_heads=heads_per_step, unroll=unroll),
        grid=(batch, GLA_HEADS // heads_per_step),
        in_specs=[pl.BlockSpec((seq, dk), lambda b, h: (b, RAW_GQ // dk + h)),
                  pl.BlockSpec((seq, dk), lambda b, h: (b, RAW_GK // dk + h)),
                  pl.BlockSpec((seq, dv), lambda b, h: (b, RAW_GV // dv + h)),
                  pl.BlockSpec((seq, dk), lambda b, h: (b, h)),
                  pl.BlockSpec((seq, dv), lambda b, h: (b, h)),
                  pl.BlockSpec((1, GLA_HEAD_V), lambda b, h: (0, 0))],
        out_specs=pl.BlockSpec((seq, dv), lambda b, h: (b, h)),
        out_shape=jax.ShapeDtypeStruct((t, GLA_HEADS * GLA_HEAD_V), BF16),
        scratch_shapes=[pltpu.VMEM((heads_per_step, GLA_HEAD_V, GLA_HEAD_K), F32)],
        compiler_params=_params(2),
        name="gla",
    )(raw, raw, raw, bcum, act, gla_norm)


def _merge_kernel(x_ref, of_ref, os_ref, og_ref, g0_ref, g1_ref, g2_ref,
                  wf_ref, ws_ref, wg_ref, wo_ref, o_ref):
    merged = (g0_ref[...].astype(F32) * _dot(of_ref[...], wf_ref[...])
              + g1_ref[...].astype(F32) * _dot(os_ref[...], ws_ref[...])
              + g2_ref[...].astype(F32) * _dot(og_ref[...], wg_ref[...]))
    o_ref[...] = x_ref[...] + _dot(merged.astype(BF16), wo_ref[...])


def _merge(x, o_fox, o_swa, o_gla, act, wf, ws, wg, wo, *, layer, tm=1024):
    t, d = x.shape
    row = lambda w, j=0: pl.BlockSpec((tm, w), lambda i: (i, j))
    return pl.pallas_call(
        _merge_kernel,
        grid=(t // tm,),
        in_specs=[row(d), row(o_fox.shape[1]), row(o_swa.shape[1]), row(o_gla.shape[1]),
                  row(d, 1), row(d, 2), row(d, 3),
                  _layer(wf.shape[1:], layer), _layer(ws.shape[1:], layer),
                  _layer(wg.shape[1:], layer), _layer(wo.shape[1:], layer)],
        out_specs=row(d),
        out_shape=jax.ShapeDtypeStruct((t, d), F32),
        compiler_params=_params(1),
        name="merge",
    )(x, o_fox, o_swa, o_gla, act, act, act, wf, ws, wg, wo)


def _w_in_pieces():
    sizes = (512, 512, 512, FOX_HEADS, 512, 128, 128, 512, 512, 1024, GLA_RANK, 1024, 3072)
    offs = [0]
    for s in sizes:
        offs.append(offs[-1] + s)
    fq, fk, fv, ff, sq, sk, sv, gq, gk, gv, glr, gr, gates = offs[:-1]
    h = HEAD_DIM
    raw = [(fq, 512), (fk, 512), (fv, 512), (sq, 512),
           (sk, h), (sk, h), (sk + h, h), (sk + h, h),
           (sv, h), (sv, h), (sv + h, h), (sv + h, h),
           (gq, 512), (gk, 512), (gv, 1024)]
    moves, dst = [], 0
    for src, width in raw:
        moves.append(("raw", dst, src, width))
        dst += width
    assert dst == RAW_WIDTH
    moves.append(("act", 0, gr, ACT_WIDTH))
    moves += [("small", 0, ff, FOX_HEADS), ("small", FOX_HEADS, glr, GLA_RANK)]
    return moves


def _w_in_kernel(w_ref, raw_ref, act_ref, small_ref):
    outs = {"raw": raw_ref, "act": act_ref, "small": small_ref}
    small_ref[...] = jnp.zeros(small_ref.shape, BF16)
    for name, dst, src, width in _w_in_pieces():
        outs[name][dst:dst + width, :] = w_ref[src:src + width, :].astype(BF16)


def _prep_w_in(w_in, *, cols=256):
    depth, d, d_in = w_in.shape
    w_t = jnp.swapaxes(w_in, 1, 2)
    out = lambda n: pl.BlockSpec((None, n, cols), lambda l, i: (l, 0, i))
    return pl.pallas_call(
        _w_in_kernel,
        grid=(depth, d // cols),
        in_specs=[pl.BlockSpec((None, d_in, cols), lambda l, i: (l, 0, i))],
        out_specs=[out(RAW_WIDTH), out(ACT_WIDTH), out(SMALL_WIDTH)],
        out_shape=[jax.ShapeDtypeStruct((depth, RAW_WIDTH, d), BF16),
                   jax.ShapeDtypeStruct((depth, ACT_WIDTH, d), BF16),
                   jax.ShapeDtypeStruct((depth, SMALL_WIDTH, d), BF16)],
        compiler_params=_params(2),
        name="w_in_layout",
    )(w_t)


def kernel(x, ffn1_norm, ffn1_w_gu, ffn1_w_down, mix_norm, w_in, fox_forget_bias, swa_sinks,
           gla_gate_w2, gla_gate_bias, gla_norm, w_branch_fox, w_branch_swa, w_branch_gla, w_out,
           ffn2_norm, ffn2_w_gu, ffn2_w_down, final_norm):
    batch, seq, d = x.shape
    depth = w_in.shape[0]
    xt = x.reshape(batch * seq, d)

    w_raw, w_act, w_small = _prep_w_in(w_in)
    w2p = jnp.zeros((depth, SMALL_WIDTH, gla_gate_w2.shape[-1]), BF16)
    w2p = w2p.at[:, FOX_HEADS:FOX_HEADS + GLA_RANK, :].set(gla_gate_w2.astype(BF16))
    fbias = jnp.zeros((depth, 1, SMALL_WIDTH), F32).at[:, 0, :FOX_HEADS].set(fox_forget_bias)
    f1gu, f1d = ffn1_w_gu.astype(BF16), ffn1_w_down.astype(BF16)
    f2gu, f2d = ffn2_w_gu.astype(BF16), ffn2_w_down.astype(BF16)
    wbf, wbs = w_branch_fox.astype(BF16), w_branch_swa.astype(BF16)
    wbg, wo = w_branch_gla.astype(BF16), w_out.astype(BF16)
    fw = final_norm.reshape(1, d)

    for l in range(depth):
        xt = _ffn(xt, ffn1_norm[l].reshape(1, d), f1gu, f1d, fw, layer=l, final_norm=False)
        raw, act, logf, bcum = _inproj(xt, mix_norm[l].reshape(1, d), w_raw, w_act, w_small, w2p,
                                       fbias[l], gla_gate_bias[l].reshape(1, -1), layer=l)
        o_fox = _fox(raw, logf, batch=batch, seq=seq)
        o_swa = _swa(raw, swa_sinks[l], batch=batch, seq=seq)
        o_gla = _gla(raw, bcum, act, gla_norm[l].reshape(1, -1), batch=batch, seq=seq)
        xt = _merge(xt, o_fox, o_swa, o_gla, act, wbf, wbs, wbg, wo, layer=l)
        xt = _ffn(xt, ffn2_norm[l].reshape(1, d), f2gu, f2d, fw, layer=l,
                  final_norm=(l == depth - 1))
    return xt.reshape(batch, seq, d)
```

```python
import functools

import jax
import jax.numpy as jnp
from jax import lax
from jax.experimental import pallas as pl
from jax.experimental.pallas import tpu as pltpu

F32 = jnp.float32
BF16 = jnp.bfloat16

EPS = 1e-6
NEG_INF = -1e30
HEAD_DIM = 64
FOX_HEADS = 8
SWA_Q_HEADS = 8
SWA_KV_HEADS = 2
WINDOW = 128
GLA_HEADS = 4
GLA_HEAD_K = 128
GLA_HEAD_V = 256
GLA_RANK = 16
GLA_TAU = 16.0
GLA_CHUNK = 64
GLA_SUPER = 256
FOX_Q_TILE = 1024
LOG2E = 1.4426950408889634
FOX_Q_SCALE = HEAD_DIM ** -0.5 * LOG2E
SWA_UNROLL = 3
LANES = 128
VMEM_LIMIT_BYTES = 56 * 1024 * 1024

RAW_FQ, RAW_FK, RAW_FV = 0, 512, 1024
RAW_SQ = 1536
RAW_GQ, RAW_GK, RAW_GV = 2048, 2560, 3072
RAW_SK, RAW_SV = 4096, 4224
RAW_WIDTH = 4352
SWA_HEAD_ORDER = (0, 4, 1, 5, 2, 6, 3, 7)
ACT_WIDTH = 4096
SMALL_WIDTH = 128


def _params(n_grid_axes):
    return pltpu.CompilerParams(
        dimension_semantics=("arbitrary",) * n_grid_axes,
        vmem_limit_bytes=VMEM_LIMIT_BYTES)


def _resident(shape):
    nd = len(shape)
    return pl.BlockSpec(shape, lambda *_: (0,) * nd, pipeline_mode=pl.Buffered(1))


def _layer(shape, layer):
    nd = len(shape)
    return pl.BlockSpec((None,) + tuple(shape), lambda *_: (layer,) + (0,) * nd,
                        pipeline_mode=pl.Buffered(1))


def _rmsnorm(x, w):
    return x * lax.rsqrt(jnp.mean(x * x, axis=-1, keepdims=True) + EPS) * w


def _sigmoid(x):
    return 0.5 * jnp.tanh(0.5 * x) + 0.5


def _silu(x):
    h = 0.5 * x
    return h + h * jnp.tanh(h)


def _log_sigmoid(x):
    return jnp.minimum(x, 0.0) - jnp.log1p(jnp.exp(-jnp.abs(x)))


def _split3(x):
    p1 = x.astype(BF16)
    r1 = x - p1.astype(F32)
    p2 = r1.astype(BF16)
    p3 = (r1 - p2.astype(F32)).astype(BF16)
    return p1, p2, p3


def _lane_repeat(x, n):
    return jnp.concatenate([x] * n, axis=1)


def _dot(a, b):
    return jnp.dot(a, b, preferred_element_type=F32)


def _dot_nt(a, b):
    return lax.dot_general(a, b, (((1,), (1,)), ((), ())), preferred_element_type=F32)


def _dot_tn(a, b):
    return lax.dot_general(a, b, (((0,), (0,)), ((), ())), preferred_element_type=F32)


def _exact_tri_cumsum(tri, x):
    p1, p2, p3 = _split3(x)
    return _dot(tri, p1) + _dot(tri, p2) + _dot(tri, p3)


def _ffn_kernel(x_ref, nw_ref, wgu_ref, wd_ref, fw_ref, o_ref, h_ref, a_ref, *,
                f_chunk, final_norm):
    h_ref[...] = _rmsnorm(x_ref[...], nw_ref[...]).astype(BF16)
    d_ff = wd_ref.shape[0]
    for c in range(d_ff // f_chunk):
        sl = slice(c * f_chunk, (c + 1) * f_chunk)
        g = _dot(h_ref[...], wgu_ref[:, sl])
        u = _dot(h_ref[...], wgu_ref[:, d_ff + c * f_chunk:d_ff + (c + 1) * f_chunk])
        a_ref[:, sl] = (_silu(g) * u).astype(BF16)
    y = x_ref[...] + 0.5 * _dot(a_ref[...], wd_ref[...])
    if final_norm:
        y = _rmsnorm(y, fw_ref[...])
    o_ref[...] = y


def _ffn(x, nw, wgu, wd, fw, *, layer, final_norm, tm=1024, f_chunk=256):
    t, d = x.shape
    d_ff = wd.shape[1]
    row = pl.BlockSpec((tm, d), lambda i: (i, 0))
    return pl.pallas_call(
        functools.partial(_ffn_kernel, f_chunk=f_chunk, final_norm=final_norm),
        grid=(t // tm,),
        in_specs=[row, _resident((1, d)), _layer((d, 2 * d_ff), layer),
                  _layer((d_ff, d), layer), _resident((1, d))],
        out_specs=row,
        out_shape=jax.ShapeDtypeStruct((t, d), F32),
        scratch_shapes=[pltpu.VMEM((tm, d), BF16), pltpu.VMEM((tm, d_ff), BF16)],
        compiler_params=_params(1),
        name="ffn",
    )(x, nw, wgu, wd, fw)


def _inproj_kernel(x_ref, nw_ref, wraw_ref, wact_ref, wsm_ref, w2_ref, fb_ref, gb_ref,
                   raw_ref, act_ref, logf_ref, bcum_ref, h_ref, *, n_chunk, gr_width):
    tm = x_ref.shape[0]
    h_ref[...] = _rmsnorm(x_ref[...], nw_ref[...]).astype(BF16)
    for start in range(0, raw_ref.shape[1], n_chunk):
        sl = slice(start, min(start + n_chunk, raw_ref.shape[1]))
        z = _dot_nt(h_ref[...], wraw_ref[sl, :])
        if sl.stop <= RAW_FK:
            z = z * FOX_Q_SCALE
        elif RAW_GQ <= sl.start and sl.stop <= RAW_GK:
            z = z * (GLA_HEAD_K ** -0.5)
        raw_ref[:, sl] = z.astype(BF16)
    for c in range(act_ref.shape[1] // n_chunk):
        sl = slice(c * n_chunk, (c + 1) * n_chunk)
        z = _dot_nt(h_ref[...], wact_ref[sl, :])
        if c * n_chunk < gr_width:
            act_ref[:, sl] = _silu(z).astype(BF16)
        else:
            act_ref[:, sl] = _sigmoid(z).astype(BF16)
    small = _dot_nt(h_ref[...], wsm_ref[...])
    logf_ref[...] = _log_sigmoid(small + fb_ref[...])
    gate_logit = _dot(small.astype(BF16), w2_ref[...]) + gb_ref[...]
    log_a = _log_sigmoid(gate_logit) * (LOG2E / GLA_TAU)
    r = lax.broadcasted_iota(jnp.int32, (GLA_SUPER, GLA_SUPER), 0)
    c = lax.broadcasted_iota(jnp.int32, (GLA_SUPER, GLA_SUPER), 1)
    tri = jnp.where((r >= c) & ((r // GLA_CHUNK) == (c // GLA_CHUNK)), 1.0, 0.0).astype(BF16)
    for s in range(tm // GLA_SUPER):
        rows = slice(s * GLA_SUPER, (s + 1) * GLA_SUPER)
        bcum_ref[rows, :] = _exact_tri_cumsum(tri, log_a[rows, :])


def _inproj(x, nw, w_raw, w_act, w_small, w2p, fbias, gbias, *, layer, tm=512, n_chunk=512):
    t, d = x.shape
    gk = w2p.shape[2]
    row = lambda w: pl.BlockSpec((tm, w), lambda i: (i, 0))
    return pl.pallas_call(
        functools.partial(_inproj_kernel, n_chunk=n_chunk, gr_width=GLA_HEADS * GLA_HEAD_V),
        grid=(t // tm,),
        in_specs=[row(d), _resident((1, d)), _layer((RAW_WIDTH, d), layer),
                  _layer((ACT_WIDTH, d), layer), _layer((SMALL_WIDTH, d), layer),
                  _layer((SMALL_WIDTH, gk), layer),
                  _resident((1, SMALL_WIDTH)), _resident((1, gk))],
        out_specs=[row(RAW_WIDTH), row(ACT_WIDTH), row(SMALL_WIDTH), row(gk)],
        out_shape=[jax.ShapeDtypeStruct((t, RAW_WIDTH), BF16),
                   jax.ShapeDtypeStruct((t, ACT_WIDTH), BF16),
                   jax.ShapeDtypeStruct((t, SMALL_WIDTH), F32),
                   jax.ShapeDtypeStruct((t, gk), F32)],
        scratch_shapes=[pltpu.VMEM((tm, d), BF16)],
        compiler_params=_params(1),
        name="inproj",
    )(x, nw, w_raw, w_act, w_small, w2p, fbias, gbias)


def _fox_kernel(q_ref, k_ref, v_ref, logf_ref, o_ref,
                p_ref, qa_ref, qb_ref, ka_ref, kb_ref, va_ref, vb_ref, m_ref, acc_ref, *, tq):
    s_len = q_ref.shape[0]
    pair = pl.program_id(1)
    blk = LANES
    n_heads = FOX_HEADS
    one_lane = 3 * n_heads
    lane = lax.broadcasted_iota(jnp.int32, (1, LANES), 1)

    @pl.when(pair == 0)
    def _():
        r = lax.broadcasted_iota(jnp.int32, (blk, blk), 0)
        cc = lax.broadcasted_iota(jnp.int32, (blk, blk), 1)
        tri = jnp.where(r >= cc, 1.0, 0.0).astype(BF16)
        ones = jnp.where(lane == one_lane, 1.0, 0.0)
        carry = jnp.zeros((1, LANES), F32)
        for n in range(s_len // blk):
            rows = slice(n * blk, (n + 1) * blk)
            within = _exact_tri_cumsum(tri, logf_ref[rows, :]) + carry
            carry = within[blk - 1:blk, :]
            p1, p2, p3 = (p.astype(F32) for p in _split3(jnp.where(lane < n_heads,
                                                                    within * LOG2E, 0.0)))
            packed = (p1 + pltpu.roll(p2, n_heads, axis=1) + pltpu.roll(p3, 2 * n_heads, axis=1)
                      + ones)
            p_ref[rows, :] = packed.astype(BF16)

    row = lax.broadcasted_iota(jnp.int32, (LANES, 2 * LANES), 0)
    col = lax.broadcasted_iota(jnp.int32, (LANES, 2 * LANES), 1)
    k_side = col >= LANES
    dst = col % LANES
    aug = dst % HEAD_DIM
    head = 2 * pair + jnp.where(dst < HEAD_DIM, 1, 0)
    part = aug - jnp.where(k_side, 3, 0)
    from_c = (part >= 0) & (part < 3) & (row == head + n_heads * part)
    ones_pos = aug - jnp.where(k_side, 0, 3)
    from_one = (row == one_lane) & (ones_pos >= 0) & (ones_pos < 3)
    sel = jnp.where(from_c, jnp.where(k_side, -1.0, 1.0), jnp.where(from_one, 1.0, 0.0))
    spare = _dot(p_ref[...], sel.astype(BF16))
    q_spare = spare[:, :LANES].astype(BF16)
    k_spare = spare[:, LANES:].astype(BF16)

    low = lax.broadcasted_iota(jnp.int32, (s_len, LANES), 1) < HEAD_DIM
    q, k, v = q_ref[...], k_ref[...], v_ref[...]
    qa_ref[...] = jnp.where(low, q, q_spare)
    qb_ref[...] = jnp.where(low, q_spare, q)
    ka_ref[...] = jnp.where(low, k, k_spare)
    kb_ref[...] = jnp.where(low, k_spare, k)
    va_ref[...] = jnp.where(low, v, jnp.where(lane == HEAD_DIM, 1.0, 0.0).astype(BF16))
    vb_ref[...] = jnp.where(low, jnp.where(lane == 0, 1.0, 0.0).astype(BF16), v)

    heads = ((qa_ref, ka_ref, va_ref), (qb_ref, kb_ref, vb_ref))

    def scores(h, q0, nq, k_rows, bias):
        q_ref_h, k_ref_h, _ = heads[h]
        s = _dot_nt(q_ref_h[q0:q0 + nq, :], k_ref_h[k_rows, :])
        return s if bias is None else s + bias

    def start(h, q0, nq, k_rows, bias):
        s = scores(h, q0, nq, k_rows, bias)
        m = jnp.max(s, axis=-1, keepdims=True) + jnp.zeros((nq, LANES), F32)
        p = jnp.exp2(s - _lane_repeat(m, s.shape[1] // LANES))
        acc_ref[h, q0:q0 + nq, :] = _dot(p.astype(BF16), heads[h][2][k_rows, :])
        m_ref[h, q0:q0 + nq, :] = m

    def update(h, q0, nq, k_rows):
        s = scores(h, q0, nq, k_rows, None)
        m_prev = m_ref[h, q0:q0 + nq, :]
        m_new = jnp.maximum(m_prev, jnp.max(s, axis=-1, keepdims=True))
        alpha = jnp.exp2(m_prev - m_new)
        p = jnp.exp2(s - _lane_repeat(m_new, s.shape[1] // LANES))
        acc_ref[h, q0:q0 + nq, :] = (alpha * acc_ref[h, q0:q0 + nq, :]
                                     + _dot(p.astype(BF16), heads[h][2][k_rows, :]))
        m_ref[h, q0:q0 + nq, :] = m_new

    half = tq // 2
    ri = lax.broadcasted_iota(jnp.int32, (half, tq), 0)
    ci = lax.broadcasted_iota(jnp.int32, (half, tq), 1)
    upper_bias = jnp.where(lax.broadcasted_iota(jnp.int32, (half, half), 1)
                           <= lax.broadcasted_iota(jnp.int32, (half, half), 0), 0.0, NEG_INF)
    lower_bias = jnp.where(ci <= ri + half, 0.0, NEG_INF)
    n_tiles = s_len // tq
    for qi in range(n_tiles):
        q0 = qi * tq
        for h in range(2):
            start(h, q0, half, slice(q0, q0 + half), upper_bias)
            start(h, q0 + half, half, slice(q0, q0 + tq), lower_bias)

    for qi in range(1, n_tiles):
        def kv_step(kj, _, q0=qi * tq):
            k_rows = pl.ds(pl.multiple_of(kj * tq, tq), tq)
            for h in range(2):
                update(h, q0, tq, k_rows)
            return 0

        lax.fori_loop(0, qi, kv_step, 0)

    acc_a, acc_b = acc_ref[0], acc_ref[1]
    l_a = jnp.sum(jnp.where(lane == HEAD_DIM, acc_a, 0.0), axis=-1, keepdims=True)
    l_b = jnp.sum(jnp.where(lane == 0, acc_b, 0.0), axis=-1, keepdims=True)
    o_ref[...] = jnp.where(lane < HEAD_DIM, acc_a / l_a, acc_b / l_b).astype(BF16)


def _fox(raw, logf, *, batch, seq):
    t = raw.shape[0]
    tq = min(FOX_Q_TILE, seq)
    n_pairs = FOX_HEADS // 2
    blk = lambda off: pl.BlockSpec((seq, LANES), lambda b, j: (b, off // LANES + j))
    return pl.pallas_call(
        functools.partial(_fox_kernel, tq=tq),
        grid=(batch, n_pairs),
        in_specs=[blk(RAW_FQ), blk(RAW_FK), blk(RAW_FV),
                  pl.BlockSpec((seq, LANES), lambda b, j: (b, 0))],
        out_specs=pl.BlockSpec((seq, LANES), lambda b, j: (b, j)),
        out_shape=jax.ShapeDtypeStruct((t, FOX_HEADS * HEAD_DIM), BF16),
        scratch_shapes=[pltpu.VMEM((seq, LANES), BF16)] * 7
                       + [pltpu.VMEM((2, seq, LANES), F32)] * 2,
        compiler_params=_params(2),
        name="fox",
    )(raw, raw, raw, logf)


def _swa_kernel(sink_ref, q_ref, k_ref, v_ref, o_ref):
    s_len = q_ref.shape[0]
    w = WINDOW
    group = SWA_Q_HEADS // SWA_KV_HEADS
    scale = HEAD_DIM ** -0.5 * LOG2E
    low = lax.broadcasted_iota(jnp.int32, (w, LANES), 1) < HEAD_DIM
    i = lax.broadcasted_iota(jnp.int32, (w, 2 * w), 0)
    j = lax.broadcasted_iota(jnp.int32, (w, 2 * w), 1)
    band = jnp.where((j > i) & (j <= i + w), 0.0, NEG_INF)
    band = jnp.concatenate([band] * group, axis=0)
    it = lax.broadcasted_iota(jnp.int32, (w, w), 0)
    jt = lax.broadcasted_iota(jnp.int32, (w, w), 1)
    tri = jnp.concatenate([jnp.where(jt <= it, 0.0, NEG_INF)] * group, axis=0)
    hrow = lax.broadcasted_iota(jnp.int32, (group * w, LANES), 0) // w

    def block(n, sinks, first):
        q_rows = pl.ds(pl.multiple_of(n * w, w), w)
        k_rows = q_rows if first else pl.ds(pl.multiple_of((n - 1) * w, w), 2 * w)
        kk = k_ref[k_rows, :]
        vv = v_ref[k_rows, :]
        q_blocks = [q_ref[q_rows, g * LANES:(g + 1) * LANES] for g in range(group)]
        outs = []
        for kvh in range(SWA_KV_HEADS):
            keep = low if kvh == 0 else jnp.logical_not(low)
            q = jnp.concatenate([jnp.where(keep, qb, jnp.zeros_like(qb)) for qb in q_blocks],
                                axis=0)
            s = _dot_nt(q, kk) * scale + (tri if first else band)
            m = jnp.maximum(jnp.max(s, axis=-1, keepdims=True), sinks[kvh])
            p = jnp.exp2(s - (m if first else _lane_repeat(m, 2 * w // LANES)))
            denom = jnp.sum(p, axis=-1, keepdims=True) + jnp.exp2(sinks[kvh] - m)
            outs.append(_dot(p.astype(BF16), vv) / denom)
        for g in range(group):
            rows = slice(g * w, (g + 1) * w)
            o_ref[q_rows, g * LANES:(g + 1) * LANES] = jnp.where(
                low, outs[0][rows, :], outs[1][rows, :]).astype(BF16)

    sinks = []
    for kvh in range(SWA_KV_HEADS):
        sink = jnp.zeros((group * w, LANES), F32)
        for g in range(group):
            sink = jnp.where(hrow == g, sink_ref[kvh * group + g] * LOG2E, sink)
        sinks.append(sink)
    block(0, sinks, True)

    def body(n, _):
        block(n, sinks, False)
        return 0

    lax.fori_loop(1, s_len // w, body, 0, unroll=SWA_UNROLL)


def _swa(raw, sinks, *, batch, seq):
    t = raw.shape[0]
    qw = SWA_Q_HEADS * HEAD_DIM
    kw = SWA_KV_HEADS * HEAD_DIM
    return pl.pallas_call(
        _swa_kernel,
        grid=(batch,),
        in_specs=[pl.BlockSpec(memory_space=pltpu.SMEM),
                  pl.BlockSpec((seq, qw), lambda b: (b, RAW_SQ // qw)),
                  pl.BlockSpec((seq, kw), lambda b: (b, RAW_SK // kw)),
                  pl.BlockSpec((seq, kw), lambda b: (b, RAW_SV // kw))],
        out_specs=pl.BlockSpec((seq, qw), lambda b: (b, 0)),
        out_shape=jax.ShapeDtypeStruct((t, qw), BF16),
        compiler_params=_params(1),
        name="swa",
    )(sinks, raw, raw, raw)


def _gla_kernel(q_ref, k_ref, v_ref, b_ref, gr_ref, nw_ref, o_ref, st_ref, *, n_heads, unroll):
    s_len = q_ref.shape[0]
    ck, sup = GLA_CHUNK, GLA_SUPER
    dk, dv = GLA_HEAD_K, GLA_HEAD_V
    per = sup // ck
    r = lax.broadcasted_iota(jnp.int32, (sup, sup), 0)
    c = lax.broadcasted_iota(jnp.int32, (sup, sup), 1)
    intra = (r >= c) & ((r // ck) == (c // ck))
    st_ref[...] = jnp.zeros(st_ref.shape, F32)

    def head_block(rows, h):
        kcols = slice(h * dk, (h + 1) * dk)
        vcols = slice(h * dv, (h + 1) * dv)
        b = b_ref[rows, kcols]
        b3 = b.reshape(per, ck, dk)
        b_last = b3[:, ck - 1:ck, :]
        qf = q_ref[rows, kcols].astype(F32)
        kf = k_ref[rows, kcols].astype(F32)
        q_t = (qf * jnp.exp2(b)).astype(BF16)
        k_t = (kf * jnp.exp2(-b)).astype(BF16)
        k_s = (kf.reshape(per, ck, dk) * jnp.exp2(b_last - b3)).astype(BF16)
        decay = jnp.exp2(b_last)
        v = v_ref[rows, vcols]
        a = jnp.where(intra, _dot_nt(q_t, k_t), 0.0)
        o_intra = _dot(a.astype(BF16), v)
        st = st_ref[h]
        outs = []
        for cidx in range(per):
            chunk = slice(cidx * ck, (cidx + 1) * ck)
            outs.append(o_intra[chunk, :] + _dot_nt(q_t[chunk, :], st.astype(BF16)))
            st = st * decay[cidx] + _dot_tn(v[chunk, :], k_s[cidx])
        st_ref[h] = st
        o = jnp.concatenate(outs, axis=0)
        o = o * lax.rsqrt(jnp.mean(o * o, axis=-1, keepdims=True) + EPS) * nw_ref[...]
        o_ref[rows, vcols] = (o * gr_ref[rows, vcols].astype(F32)).astype(BF16)

    def super_block(sb, _):
        rows = pl.ds(pl.multiple_of(sb * sup, sup), sup)
        for h in range(n_heads):
            head_block(rows, h)
        return 0

    lax.fori_loop(0, s_len // sup, super_block, 0, unroll=unroll)


def _gla(raw, bcum, act, gla_norm, *, batch, seq, heads_per_step=2, unroll=2):
    t = raw.shape[0]
    dk, dv = GLA_HEAD_K * heads_per_step, GLA_HEAD_V * heads_per_step
    assert RAW_GQ % dk == 0 and RAW_GK % dk == 0 and RAW_GV % dv == 0
    return pl.pallas_call(
        functools.partial(_gla_kernel, n_heads=heads_per_step, unroll=unroll),
        grid=(batch, GLA_HEADS // heads_per_step),
        in_specs=[pl.BlockSpec((seq, dk), lambda b, h: (b, RAW_GQ // dk + h)),
                  pl.BlockSpec((seq, dk), lambda b, h: (b, RAW_GK // dk + h)),
                  pl.BlockSpec((seq, dv), lambda b, h: (b, RAW_GV // dv + h)),
                  pl.BlockSpec((seq, dk), lambda b, h: (b, h)),
                  pl.BlockSpec((seq, dv), lambda b, h: (b, h)),
                  pl.BlockSpec((1, GLA_HEAD_V), lambda b, h: (0, 0))],
        out_specs=pl.BlockSpec((seq, dv), lambda b, h: (b, h)),
        out_shape=jax.ShapeDtypeStruct((t, GLA_HEADS * GLA_HEAD_V), BF16),
        scratch_shapes=[pltpu.VMEM((heads_per_step, GLA_HEAD_V, GLA_HEAD_K), F32)],
        compiler_params=_params(2),
        name="gla",
    )(raw, raw, raw, bcum, act, gla_norm)


def _merge_kernel(x_ref, of_ref, os_ref, og_ref, g0_ref, g1_ref, g2_ref,
                  wf_ref, ws_ref, wg_ref, wo_ref, o_ref):
    merged = (g0_ref[...].astype(F32) * _dot(of_ref[...], wf_ref[...])
              + g1_ref[...].astype(F32) * _dot(os_ref[...], ws_ref[...])
              + g2_ref[...].astype(F32) * _dot(og_ref[...], wg_ref[...]))
    o_ref[...] = x_ref[...] + _dot(merged.astype(BF16), wo_ref[...])


def _merge(x, o_fox, o_swa, o_gla, act, wf, ws, wg, wo, *, layer, tm=1024):
    t, d = x.shape
    row = lambda w, j=0: pl.BlockSpec((tm, w), lambda i: (i, j))
    return pl.pallas_call(
        _merge_kernel,
        grid=(t // tm,),
        in_specs=[row(d), row(o_fox.shape[1]), row(o_swa.shape[1]), row(o_gla.shape[1]),
                  row(d, 1), row(d, 2), row(d, 3),
                  _layer(wf.shape[1:], layer), _layer(ws.shape[1:], layer),
                  _layer(wg.shape[1:], layer), _layer(wo.shape[1:], layer)],
        out_specs=row(d),
        out_shape=jax.ShapeDtypeStruct((t, d), F32),
        compiler_params=_params(1),
        name="merge",
    )(x, o_fox, o_swa, o_gla, act, act, act, wf, ws, wg, wo)


def _w_in_pieces():
    sizes = (512, 512, 512, FOX_HEADS, 512, 128, 128, 512, 512, 1024, GLA_RANK, 1024, 3072)
    offs = [0]
    for s in sizes:
        offs.append(offs[-1] + s)
    fq, fk, fv, ff, sq, sk, sv, gq, gk, gv, glr, gr, gates = offs[:-1]
    h = HEAD_DIM
    raw = ([(fq, 512), (fk, 512), (fv, 512)] + [(sq + h * i, h) for i in SWA_HEAD_ORDER]
           + [(gq, 512), (gk, 512), (gv, 1024), (sk, 2 * h), (sv, 2 * h)])
    moves, dst = [], 0
    for src, width in raw:
        moves.append(("raw", dst, src, width))
        dst += width
    assert dst == RAW_WIDTH
    moves.append(("act", 0, gr, ACT_WIDTH))
    moves += [("small", 0, ff, FOX_HEADS), ("small", FOX_HEADS, glr, GLA_RANK)]
    return moves


def _w_in_kernel(w_ref, raw_ref, act_ref, small_ref):
    outs = {"raw": raw_ref, "act": act_ref, "small": small_ref}
    small_ref[...] = jnp.zeros(small_ref.shape, BF16)
    for name, dst, src, width in _w_in_pieces():
        outs[name][dst:dst + width, :] = w_ref[src:src + width, :].astype(BF16)


def _prep_w_in(w_in, *, cols=256):
    depth, d, d_in = w_in.shape
    w_t = jnp.swapaxes(w_in, 1, 2)
    out = lambda n: pl.BlockSpec((None, n, cols), lambda l, i: (l, 0, i))
    return pl.pallas_call(
        _w_in_kernel,
        grid=(depth, d // cols),
        in_specs=[pl.BlockSpec((None, d_in, cols), lambda l, i: (l, 0, i))],
        out_specs=[out(RAW_WIDTH), out(ACT_WIDTH), out(SMALL_WIDTH)],
        out_shape=[jax.ShapeDtypeStruct((depth, RAW_WIDTH, d), BF16),
                   jax.ShapeDtypeStruct((depth, ACT_WIDTH, d), BF16),
                   jax.ShapeDtypeStruct((depth, SMALL_WIDTH, d), BF16)],
        compiler_params=_params(2),
        name="w_in_layout",
    )(w_t)


def kernel(x, ffn1_norm, ffn1_w_gu, ffn1_w_down, mix_norm, w_in, fox_forget_bias, swa_sinks,
           gla_gate_w2, gla_gate_bias, gla_norm, w_branch_fox, w_branch_swa, w_branch_gla, w_out,
           ffn2_norm, ffn2_w_gu, ffn2_w_down, final_norm):
    batch, seq, d = x.shape
    depth = w_in.shape[0]
    xt = x.reshape(batch * seq, d)

    w_raw, w_act, w_small = _prep_w_in(w_in)
    w2p = jnp.zeros((depth, SMALL_WIDTH, gla_gate_w2.shape[-1]), BF16)
    w2p = w2p.at[:, FOX_HEADS:FOX_HEADS + GLA_RANK, :].set(gla_gate_w2.astype(BF16))
    fbias = jnp.zeros((depth, 1, SMALL_WIDTH), F32).at[:, 0, :FOX_HEADS].set(fox_forget_bias)
    f1gu, f1d = ffn1_w_gu.astype(BF16), ffn1_w_down.astype(BF16)
    f2gu, f2d = ffn2_w_gu.astype(BF16), ffn2_w_down.astype(BF16)
    wbf = w_branch_fox.astype(BF16)
    swa_rows = w_branch_swa.reshape(depth, SWA_Q_HEADS, HEAD_DIM, d)[:, jnp.array(SWA_HEAD_ORDER)]
    wbs = swa_rows.reshape(depth, SWA_Q_HEADS * HEAD_DIM, d).astype(BF16)
    wbg, wo = w_branch_gla.astype(BF16), w_out.astype(BF16)
    fw = final_norm.reshape(1, d)

    for l in range(depth):
        xt = _ffn(xt, ffn1_norm[l].reshape(1, d), f1gu, f1d, fw, layer=l, final_norm=False)
        raw, act, logf, bcum = _inproj(xt, mix_norm[l].reshape(1, d), w_raw, w_act, w_small, w2p,
                                       fbias[l], gla_gate_bias[l].reshape(1, -1), layer=l)
        o_fox = _fox(raw, logf, batch=batch, seq=seq)
        o_swa = _swa(raw, swa_sinks[l], batch=batch, seq=seq)
        o_gla = _gla(raw, bcum, act, gla_norm[l].reshape(1, -1), batch=batch, seq=seq)
        xt = _merge(xt, o_fox, o_swa, o_gla, act, wbf, wbs, wbg, wo, layer=l)
        xt = _ffn(xt, ffn2_norm[l].reshape(1, d), f2gu, f2d, fw, layer=l,
                  final_norm=(l == depth - 1))
    return xt.reshape(batch, seq, d)
```

```python
import functools

import jax
import jax.numpy as jnp
from jax import lax
from jax.experimental import pallas as pl
from jax.experimental.pallas import tpu as pltpu

F32 = jnp.float32
BF16 = jnp.bfloat16

EPS = 1e-6
NEG_INF = -1e30
HEAD_DIM = 64
FOX_HEADS = 8
SWA_Q_HEADS = 8
SWA_KV_HEADS = 2
WINDOW = 128
GLA_HEADS = 4
GLA_HEAD_K = 128
GLA_HEAD_V = 256
GLA_RANK = 16
GLA_TAU = 16.0
GLA_CHUNK = 64
GLA_SUPER = 256
FOX_Q_TILE = 1024
LOG2E = 1.4426950408889634
FOX_ONE_LANE = 3 * FOX_HEADS
FOX_Q_SCALE = HEAD_DIM ** -0.5 * LOG2E
SWA_UNROLL = 3
LANES = 128
VMEM_LIMIT_BYTES = 56 * 1024 * 1024

RAW_FQ, RAW_FK, RAW_FV = 0, 512, 1024
RAW_SQ = 1536
RAW_GQ, RAW_GK, RAW_GV = 2048, 2560, 3072
RAW_SK, RAW_SV = 4096, 4224
RAW_WIDTH = 4352
SWA_HEAD_ORDER = (0, 4, 1, 5, 2, 6, 3, 7)
ACT_WIDTH = 4096
SMALL_WIDTH = 128


def _params(n_grid_axes):
    return pltpu.CompilerParams(
        dimension_semantics=("arbitrary",) * n_grid_axes,
        vmem_limit_bytes=VMEM_LIMIT_BYTES)


def _resident(shape):
    nd = len(shape)
    return pl.BlockSpec(shape, lambda *_: (0,) * nd, pipeline_mode=pl.Buffered(1))


def _layer(shape, layer):
    nd = len(shape)
    return pl.BlockSpec((None,) + tuple(shape), lambda *_: (layer,) + (0,) * nd,
                        pipeline_mode=pl.Buffered(1))


def _rmsnorm(x, w):
    return x * lax.rsqrt(jnp.mean(x * x, axis=-1, keepdims=True) + EPS) * w


def _sigmoid(x):
    return 0.5 * jnp.tanh(0.5 * x) + 0.5


def _silu(x):
    h = 0.5 * x
    return h + h * jnp.tanh(h)


def _log2_sigmoid(x):
    return jnp.minimum(x, 0.0) * LOG2E - jnp.log2(1.0 + jnp.exp2(jnp.abs(x) * -LOG2E))


def _split3(x):
    p1 = x.astype(BF16)
    r1 = x - p1.astype(F32)
    p2 = r1.astype(BF16)
    p3 = (r1 - p2.astype(F32)).astype(BF16)
    return p1, p2, p3


def _lane_repeat(x, n):
    return jnp.concatenate([x] * n, axis=1)


def _dot(a, b):
    return jnp.dot(a, b, preferred_element_type=F32)


def _dot_nt(a, b):
    return lax.dot_general(a, b, (((1,), (1,)), ((), ())), preferred_element_type=F32)


def _dot_tn(a, b):
    return lax.dot_general(a, b, (((0,), (0,)), ((), ())), preferred_element_type=F32)


def _exact_tri_cumsum(tri, x):
    p1, p2, p3 = _split3(x)
    return _dot(tri, p1) + _dot(tri, p2) + _dot(tri, p3)


def _ffn_kernel(x_ref, nw_ref, wgu_ref, wd_ref, fw_ref, o_ref, h_ref, a_ref, *,
                f_chunk, final_norm):
    h_ref[...] = _rmsnorm(x_ref[...], nw_ref[...]).astype(BF16)
    d_ff = wd_ref.shape[0]
    for c in range(d_ff // f_chunk):
        sl = slice(c * f_chunk, (c + 1) * f_chunk)
        g = _dot(h_ref[...], wgu_ref[:, sl])
        u = _dot(h_ref[...], wgu_ref[:, d_ff + c * f_chunk:d_ff + (c + 1) * f_chunk])
        a_ref[:, sl] = (_silu(g) * u).astype(BF16)
    y = x_ref[...] + 0.5 * _dot(a_ref[...], wd_ref[...])
    if final_norm:
        y = _rmsnorm(y, fw_ref[...])
    o_ref[...] = y


def _ffn(x, nw, wgu, wd, fw, *, layer, final_norm, tm=1024, f_chunk=256):
    t, d = x.shape
    d_ff = wd.shape[1]
    row = pl.BlockSpec((tm, d), lambda i: (i, 0))
    return pl.pallas_call(
        functools.partial(_ffn_kernel, f_chunk=f_chunk, final_norm=final_norm),
        grid=(t // tm,),
        in_specs=[row, _resident((1, d)), _layer((d, 2 * d_ff), layer),
                  _layer((d_ff, d), layer), _resident((1, d))],
        out_specs=row,
        out_shape=jax.ShapeDtypeStruct((t, d), F32),
        scratch_shapes=[pltpu.VMEM((tm, d), BF16), pltpu.VMEM((tm, d_ff), BF16)],
        compiler_params=_params(1),
        name="ffn",
    )(x, nw, wgu, wd, fw)


def _inproj_kernel(x_ref, nw_ref, wraw_ref, wact_ref, wsm_ref,
                   raw_ref, act_ref, small_t_ref, h_ref, *, n_chunk, gr_width):
    h_ref[...] = _rmsnorm(x_ref[...], nw_ref[...]).astype(BF16)
    for start in range(0, raw_ref.shape[1], n_chunk):
        sl = slice(start, min(start + n_chunk, raw_ref.shape[1]))
        z = _dot_nt(h_ref[...], wraw_ref[sl, :])
        if sl.stop <= RAW_FK:
            z = z * FOX_Q_SCALE
        elif RAW_GQ <= sl.start and sl.stop <= RAW_GK:
            z = z * (GLA_HEAD_K ** -0.5)
        raw_ref[:, sl] = z.astype(BF16)
    for c in range(act_ref.shape[1] // n_chunk):
        sl = slice(c * n_chunk, (c + 1) * n_chunk)
        z = _dot_nt(h_ref[...], wact_ref[sl, :])
        if c * n_chunk < gr_width:
            act_ref[:, sl] = _silu(z).astype(BF16)
        else:
            act_ref[:, sl] = _sigmoid(z).astype(BF16)
    small_t_ref[...] = _dot_nt(wsm_ref[...], h_ref[...])


def _inproj(x, nw, w_raw, w_act, w_small, *, layer, tm=512, n_chunk=512):
    t, d = x.shape
    row = lambda w: pl.BlockSpec((tm, w), lambda i: (i, 0))
    return pl.pallas_call(
        functools.partial(_inproj_kernel, n_chunk=n_chunk, gr_width=GLA_HEADS * GLA_HEAD_V),
        grid=(t // tm,),
        in_specs=[row(d), _resident((1, d)), _layer((RAW_WIDTH, d), layer),
                  _layer((ACT_WIDTH, d), layer), _layer((SMALL_WIDTH, d), layer)],
        out_specs=[row(RAW_WIDTH), row(ACT_WIDTH),
                   pl.BlockSpec((SMALL_WIDTH, tm), lambda i: (0, i))],
        out_shape=[jax.ShapeDtypeStruct((t, RAW_WIDTH), BF16),
                   jax.ShapeDtypeStruct((t, ACT_WIDTH), BF16),
                   jax.ShapeDtypeStruct((SMALL_WIDTH, t), F32)],
        scratch_shapes=[pltpu.VMEM((tm, d), BF16)],
        compiler_params=_params(1),
        name="inproj",
    )(x, nw, w_raw, w_act, w_small)


def _decay_kernel(small_t_ref, w2t_ref, fbt_ref, gbt_ref, p_ref, bcum_ref):
    s_len = small_t_ref.shape[1]
    blk = LANES
    n_heads = FOX_HEADS
    r = lax.broadcasted_iota(jnp.int32, (blk, blk), 0)
    cc = lax.broadcasted_iota(jnp.int32, (blk, blk), 1)
    upper = jnp.where(r <= cc, 1.0, 0.0).astype(BF16)
    pad = jnp.zeros((blk - 3 * n_heads, blk), F32)
    carry = jnp.zeros((n_heads, blk), F32)
    for n in range(s_len // blk):
        cols = slice(n * blk, (n + 1) * blk)
        log2_f = _log2_sigmoid(small_t_ref[0:n_heads, cols] + fbt_ref[...])
        p1, p2, p3 = _split3(log2_f)
        within = _dot(p1, upper) + _dot(p2, upper) + _dot(p3, upper) + carry
        carry = jnp.broadcast_to(within[:, blk - 1:blk], (n_heads, blk))
        c1, c2, c3 = (p.astype(F32) for p in _split3(within))
        stacked = jnp.concatenate([c1, c2, c3, pad], axis=0)
        stacked = jnp.where(r == FOX_ONE_LANE, 1.0, stacked)
        p_ref[cols, :] = stacked.T.astype(BF16)

    r = lax.broadcasted_iota(jnp.int32, (GLA_SUPER, GLA_SUPER), 0)
    cc = lax.broadcasted_iota(jnp.int32, (GLA_SUPER, GLA_SUPER), 1)
    upper = jnp.where((r <= cc) & ((r // GLA_CHUNK) == (cc // GLA_CHUNK)), 1.0, 0.0).astype(BF16)
    gate_bias = _lane_repeat(gbt_ref[...], GLA_SUPER // LANES)

    def super_block(sb, _):
        cols = pl.ds(pl.multiple_of(sb * GLA_SUPER, GLA_SUPER), GLA_SUPER)
        gate_logit = _dot(w2t_ref[...], small_t_ref[:, cols].astype(BF16)) + gate_bias
        log2_a = _log2_sigmoid(gate_logit) * (1.0 / GLA_TAU)
        hi = log2_a.astype(BF16)
        lo = (log2_a - hi.astype(F32)).astype(BF16)
        bcum_ref[cols, :] = (_dot(hi, upper) + _dot(lo, upper)).T
        return 0

    lax.fori_loop(0, s_len // GLA_SUPER, super_block, 0, unroll=2)


def _decay(small_t, w2t, fbias_t, gbias_t, *, batch, seq, layer):
    t = small_t.shape[1]
    gk = w2t.shape[1]
    return pl.pallas_call(
        _decay_kernel,
        grid=(batch,),
        in_specs=[pl.BlockSpec((SMALL_WIDTH, seq), lambda b: (0, b)),
                  _layer((gk, SMALL_WIDTH), layer), _layer((FOX_HEADS, LANES), layer),
                  _layer((gk, LANES), layer)],
        out_specs=[pl.BlockSpec((seq, LANES), lambda b: (b, 0)),
                   pl.BlockSpec((seq, gk), lambda b: (b, 0))],
        out_shape=[jax.ShapeDtypeStruct((t, LANES), BF16), jax.ShapeDtypeStruct((t, gk), F32)],
        compiler_params=_params(1),
        name="decay",
    )(small_t, w2t, fbias_t, gbias_t)


def _fox_kernel(q_ref, k_ref, v_ref, p_ref, o_ref,
                qa_ref, qb_ref, ka_ref, kb_ref, va_ref, vb_ref, m_ref, acc_ref, *, tq):
    s_len = q_ref.shape[0]
    pair = pl.program_id(1)
    n_heads = FOX_HEADS
    one_lane = FOX_ONE_LANE
    lane = lax.broadcasted_iota(jnp.int32, (1, LANES), 1)

    row = lax.broadcasted_iota(jnp.int32, (LANES, 2 * LANES), 0)
    col = lax.broadcasted_iota(jnp.int32, (LANES, 2 * LANES), 1)
    k_side = col >= LANES
    dst = col % LANES
    aug = dst % HEAD_DIM
    head = 2 * pair + jnp.where(dst < HEAD_DIM, 1, 0)
    part = aug - jnp.where(k_side, 3, 0)
    from_c = (part >= 0) & (part < 3) & (row == head + n_heads * part)
    ones_pos = aug - jnp.where(k_side, 0, 3)
    from_one = (row == one_lane) & (ones_pos >= 0) & (ones_pos < 3)
    sel = jnp.where(from_c, jnp.where(k_side, -1.0, 1.0), jnp.where(from_one, 1.0, 0.0))
    spare = _dot(p_ref[...], sel.astype(BF16))
    q_spare = spare[:, :LANES].astype(BF16)
    k_spare = spare[:, LANES:].astype(BF16)

    low = lax.broadcasted_iota(jnp.int32, (s_len, LANES), 1) < HEAD_DIM
    q, k, v = q_ref[...], k_ref[...], v_ref[...]
    qa_ref[...] = jnp.where(low, q, q_spare)
    qb_ref[...] = jnp.where(low, q_spare, q)
    ka_ref[...] = jnp.where(low, k, k_spare)
    kb_ref[...] = jnp.where(low, k_spare, k)
    va_ref[...] = jnp.where(low, v, jnp.where(lane == HEAD_DIM, 1.0, 0.0).astype(BF16))
    vb_ref[...] = jnp.where(low, jnp.where(lane == 0, 1.0, 0.0).astype(BF16), v)

    heads = ((qa_ref, ka_ref, va_ref), (qb_ref, kb_ref, vb_ref))

    def scores(h, q0, nq, k_rows, bias):
        q_ref_h, k_ref_h, _ = heads[h]
        s = _dot_nt(q_ref_h[q0:q0 + nq, :], k_ref_h[k_rows, :])
        return s if bias is None else s + bias

    def start(h, q0, nq, k_rows, bias):
        s = scores(h, q0, nq, k_rows, bias)
        m = jnp.max(s, axis=-1, keepdims=True) + jnp.zeros((nq, LANES), F32)
        p = jnp.exp2(s - _lane_repeat(m, s.shape[1] // LANES))
        acc_ref[h, q0:q0 + nq, :] = _dot(p.astype(BF16), heads[h][2][k_rows, :])
        m_ref[h, q0:q0 + nq, :] = m

    def update(h, q0, nq, k_rows):
        s = scores(h, q0, nq, k_rows, None)
        m_prev = m_ref[h, q0:q0 + nq, :]
        m_new = jnp.maximum(m_prev, jnp.max(s, axis=-1, keepdims=True))
        alpha = jnp.exp2(m_prev - m_new)
        p = jnp.exp2(s - _lane_repeat(m_new, s.shape[1] // LANES))
        acc_ref[h, q0:q0 + nq, :] = (alpha * acc_ref[h, q0:q0 + nq, :]
                                     + _dot(p.astype(BF16), heads[h][2][k_rows, :]))
        m_ref[h, q0:q0 + nq, :] = m_new

    half = tq // 2
    ri = lax.broadcasted_iota(jnp.int32, (half, tq), 0)
    ci = lax.broadcasted_iota(jnp.int32, (half, tq), 1)
    upper_bias = jnp.where(lax.broadcasted_iota(jnp.int32, (half, half), 1)
                           <= lax.broadcasted_iota(jnp.int32, (half, half), 0), 0.0, NEG_INF)
    lower_bias = jnp.where(ci <= ri + half, 0.0, NEG_INF)
    n_tiles = s_len // tq
    for qi in range(n_tiles):
        q0 = qi * tq
        for h in range(2):
            start(h, q0, half, slice(q0, q0 + half), upper_bias)
            start(h, q0 + half, half, slice(q0, q0 + tq), lower_bias)

    for qi in range(1, n_tiles):
        def kv_step(kj, _, q0=qi * tq):
            k_rows = pl.ds(pl.multiple_of(kj * tq, tq), tq)
            for h in range(2):
                update(h, q0, tq, k_rows)
            return 0

        lax.fori_loop(0, qi, kv_step, 0)

    acc_a, acc_b = acc_ref[0], acc_ref[1]
    l_a = jnp.sum(jnp.where(lane == HEAD_DIM, acc_a, 0.0), axis=-1, keepdims=True)
    l_b = jnp.sum(jnp.where(lane == 0, acc_b, 0.0), axis=-1, keepdims=True)
    o_ref[...] = jnp.where(lane < HEAD_DIM, acc_a / l_a, acc_b / l_b).astype(BF16)


def _fox(raw, c_split, *, batch, seq):
    t = raw.shape[0]
    tq = min(FOX_Q_TILE, seq)
    n_pairs = FOX_HEADS // 2
    blk = lambda off: pl.BlockSpec((seq, LANES), lambda b, j: (b, off // LANES + j))
    return pl.pallas_call(
        functools.partial(_fox_kernel, tq=tq),
        grid=(batch, n_pairs),
        in_specs=[blk(RAW_FQ), blk(RAW_FK), blk(RAW_FV),
                  pl.BlockSpec((seq, LANES), lambda b, j: (b, 0))],
        out_specs=pl.BlockSpec((seq, LANES), lambda b, j: (b, j)),
        out_shape=jax.ShapeDtypeStruct((t, FOX_HEADS * HEAD_DIM), BF16),
        scratch_shapes=[pltpu.VMEM((seq, LANES), BF16)] * 6
                       + [pltpu.VMEM((2, seq, LANES), F32)] * 2,
        compiler_params=_params(2),
        name="fox",
    )(raw, raw, raw, c_split)


def _swa_kernel(sink_ref, q_ref, k_ref, v_ref, o_ref):
    s_len = q_ref.shape[0]
    w = WINDOW
    group = SWA_Q_HEADS // SWA_KV_HEADS
    scale = HEAD_DIM ** -0.5 * LOG2E
    low = lax.broadcasted_iota(jnp.int32, (w, LANES), 1) < HEAD_DIM
    i = lax.broadcasted_iota(jnp.int32, (w, 2 * w), 0)
    j = lax.broadcasted_iota(jnp.int32, (w, 2 * w), 1)
    band = jnp.where((j > i) & (j <= i + w), 0.0, NEG_INF)
    band = jnp.concatenate([band] * group, axis=0)
    it = lax.broadcasted_iota(jnp.int32, (w, w), 0)
    jt = lax.broadcasted_iota(jnp.int32, (w, w), 1)
    tri = jnp.concatenate([jnp.where(jt <= it, 0.0, NEG_INF)] * group, axis=0)
    hrow = lax.broadcasted_iota(jnp.int32, (group * w, LANES), 0) // w

    def block(n, sinks, first):
        q_rows = pl.ds(pl.multiple_of(n * w, w), w)
        k_rows = q_rows if first else pl.ds(pl.multiple_of((n - 1) * w, w), 2 * w)
        kk = k_ref[k_rows, :]
        vv = v_ref[k_rows, :]
        q_blocks = [q_ref[q_rows, g * LANES:(g + 1) * LANES] for g in range(group)]
        outs = []
        for kvh in range(SWA_KV_HEADS):
            keep = low if kvh == 0 else jnp.logical_not(low)
            q = jnp.concatenate([jnp.where(keep, qb, jnp.zeros_like(qb)) for qb in q_blocks],
                                axis=0)
            s = _dot_nt(q, kk) * scale + (tri if first else band)
            m = jnp.maximum(jnp.max(s, axis=-1, keepdims=True), sinks[kvh])
            p = jnp.exp2(s - (m if first else _lane_repeat(m, 2 * w // LANES)))
            denom = jnp.sum(p, axis=-1, keepdims=True) + jnp.exp2(sinks[kvh] - m)
            outs.append(_dot(p.astype(BF16), vv) / denom)
        for g in range(group):
            rows = slice(g * w, (g + 1) * w)
            o_ref[q_rows, g * LANES:(g + 1) * LANES] = jnp.where(
                low, outs[0][rows, :], outs[1][rows, :]).astype(BF16)

    sinks = []
    for kvh in range(SWA_KV_HEADS):
        sink = jnp.zeros((group * w, LANES), F32)
        for g in range(group):
            sink = jnp.where(hrow == g, sink_ref[kvh * group + g] * LOG2E, sink)
        sinks.append(sink)
    block(0, sinks, True)

    def body(n, _):
        block(n, sinks, False)
        return 0

    lax.fori_loop(1, s_len // w, body, 0, unroll=SWA_UNROLL)


def _swa(raw, sinks, *, batch, seq):
    t = raw.shape[0]
    qw = SWA_Q_HEADS * HEAD_DIM
    kw = SWA_KV_HEADS * HEAD_DIM
    return pl.pallas_call(
        _swa_kernel,
        grid=(batch,),
        in_specs=[pl.BlockSpec(memory_space=pltpu.SMEM),
                  pl.BlockSpec((seq, qw), lambda b: (b, RAW_SQ // qw)),
                  pl.BlockSpec((seq, kw), lambda b: (b, RAW_SK // kw)),
                  pl.BlockSpec((seq, kw), lambda b: (b, RAW_SV // kw))],
        out_specs=pl.BlockSpec((seq, qw), lambda b: (b, 0)),
        out_shape=jax.ShapeDtypeStruct((t, qw), BF16),
        compiler_params=_params(1),
        name="swa",
    )(sinks, raw, raw, raw)


def _gla_kernel(q_ref, k_ref, v_ref, b_ref, gr_ref, nw_ref, o_ref, st_ref, *, n_heads, unroll):
    s_len = q_ref.shape[0]
    ck, sup = GLA_CHUNK, GLA_SUPER
    dk, dv = GLA_HEAD_K, GLA_HEAD_V
    per = sup // ck
    r = lax.broadcasted_iota(jnp.int32, (sup, sup), 0)
    c = lax.broadcasted_iota(jnp.int32, (sup, sup), 1)
    intra = (r >= c) & ((r // ck) == (c // ck))
    st_ref[...] = jnp.zeros(st_ref.shape, F32)

    def head_block(rows, h):
        kcols = slice(h * dk, (h + 1) * dk)
        vcols = slice(h * dv, (h + 1) * dv)
        b = b_ref[rows, kcols]
        b3 = b.reshape(per, ck, dk)
        b_last = b3[:, ck - 1:ck, :]
        qf = q_ref[rows, kcols].astype(F32)
        kf = k_ref[rows, kcols].astype(F32)
        q_t = (qf * jnp.exp2(b)).astype(BF16)
        k_t = (kf * jnp.exp2(-b)).astype(BF16)
        k_s = (kf.reshape(per, ck, dk) * jnp.exp2(b_last - b3)).astype(BF16)
        decay = jnp.exp2(b_last)
        v = v_ref[rows, vcols]
        a = jnp.where(intra, _dot_nt(q_t, k_t), 0.0)
        o_intra = _dot(a.astype(BF16), v)
        st = st_ref[h]
        outs = []
        for cidx in range(per):
            chunk = slice(cidx * ck, (cidx + 1) * ck)
            outs.append(o_intra[chunk, :] + _dot_nt(q_t[chunk, :], st.astype(BF16)))
            st = st * decay[cidx] + _dot_tn(v[chunk, :], k_s[cidx])
        st_ref[h] = st
        o = jnp.concatenate(outs, axis=0)
        o = o * lax.rsqrt(jnp.mean(o * o, axis=-1, keepdims=True) + EPS) * nw_ref[...]
        o_ref[rows, vcols] = (o * gr_ref[rows, vcols].astype(F32)).astype(BF16)

    def super_block(sb, _):
        rows = pl.ds(pl.multiple_of(sb * sup, sup), sup)
        for h in range(n_heads):
            head_block(rows, h)
        return 0

    lax.fori_loop(0, s_len // sup, super_block, 0, unroll=unroll)


def _gla(raw, bcum, act, gla_norm, *, batch, seq, heads_per_step=2, unroll=2):
    t = raw.shape[0]
    dk, dv = GLA_HEAD_K * heads_per_step, GLA_HEAD_V * heads_per_step
    assert RAW_GQ % dk == 0 and RAW_GK % dk == 0 and RAW_GV % dv == 0
    return pl.pallas_call(
        functools.partial(_gla_kernel, n_heads=heads_per_step, unroll=unroll),
        grid=(batch, GLA_HEADS // heads_per_step),
        in_specs=[pl.BlockSpec((seq, dk), lambda b, h: (b, RAW_GQ // dk + h)),
                  pl.BlockSpec((seq, dk), lambda b, h: (b, RAW_GK // dk + h)),
                  pl.BlockSpec((seq, dv), lambda b, h: (b, RAW_GV // dv + h)),
                  pl.BlockSpec((seq, dk), lambda b, h: (b, h)),
                  pl.BlockSpec((seq, dv), lambda b, h: (b, h)),
                  pl.BlockSpec((1, GLA_HEAD_V), lambda b, h: (0, 0))],
        out_specs=pl.BlockSpec((seq, dv), lambda b, h: (b, h)),
        out_shape=jax.ShapeDtypeStruct((t, GLA_HEADS * GLA_HEAD_V), BF16),
        scratch_shapes=[pltpu.VMEM((heads_per_step, GLA_HEAD_V, GLA_HEAD_K), F32)],
        compiler_params=_params(2),
        name="gla",
    )(raw, raw, raw, bcum, act, gla_norm)


def _merge_kernel(x_ref, of_ref, os_ref, og_ref, g0_ref, g1_ref, g2_ref,
                  wf_ref, ws_ref, wg_ref, wo_ref, o_ref):
    merged = (g0_ref[...].astype(F32) * _dot(of_ref[...], wf_ref[...])
              + g1_ref[...].astype(F32) * _dot(os_ref[...], ws_ref[...])
              + g2_ref[...].astype(F32) * _dot(og_ref[...], wg_ref[...]))
    o_ref[...] = x_ref[...] + _dot(merged.astype(BF16), wo_ref[...])


def _merge(x, o_fox, o_swa, o_gla, act, wf, ws, wg, wo, *, layer, tm=1024):
    t, d = x.shape
    row = lambda w, j=0: pl.BlockSpec((tm, w), lambda i: (i, j))
    return pl.pallas_call(
        _merge_kernel,
        grid=(t // tm,),
        in_specs=[row(d), row(o_fox.shape[1]), row(o_swa.shape[1]), row(o_gla.shape[1]),
                  row(d, 1), row(d, 2), row(d, 3),
                  _layer(wf.shape[1:], layer), _layer(ws.shape[1:], layer),
                  _layer(wg.shape[1:], layer), _layer(wo.shape[1:], layer)],
        out_specs=row(d),
        out_shape=jax.ShapeDtypeStruct((t, d), F32),
        compiler_params=_params(1),
        name="merge",
    )(x, o_fox, o_swa, o_gla, act, act, act, wf, ws, wg, wo)


def _w_in_pieces():
    sizes = (512, 512, 512, FOX_HEADS, 512, 128, 128, 512, 512, 1024, GLA_RANK, 1024, 3072)
    offs = [0]
    for s in sizes:
        offs.append(offs[-1] + s)
    fq, fk, fv, ff, sq, sk, sv, gq, gk, gv, glr, gr, gates = offs[:-1]
    h = HEAD_DIM
    raw = ([(fq, 512), (fk, 512), (fv, 512)] + [(sq + h * i, h) for i in SWA_HEAD_ORDER]
           + [(gq, 512), (gk, 512), (gv, 1024), (sk, 2 * h), (sv, 2 * h)])
    moves, dst = [], 0
    for src, width in raw:
        moves.append(("raw", dst, src, width))
        dst += width
    assert dst == RAW_WIDTH
    moves.append(("act", 0, gr, ACT_WIDTH))
    moves += [("small", 0, ff, FOX_HEADS), ("small", FOX_HEADS, glr, GLA_RANK)]
    return moves


def _w_in_kernel(w_ref, raw_ref, act_ref, small_ref):
    outs = {"raw": raw_ref, "act": act_ref, "small": small_ref}
    small_ref[...] = jnp.zeros(small_ref.shape, BF16)
    for name, dst, src, width in _w_in_pieces():
        outs[name][dst:dst + width, :] = w_ref[src:src + width, :].astype(BF16)


def _prep_w_in(w_in, *, cols=256):
    depth, d, d_in = w_in.shape
    w_t = jnp.swapaxes(w_in, 1, 2)
    out = lambda n: pl.BlockSpec((None, n, cols), lambda l, i: (l, 0, i))
    return pl.pallas_call(
        _w_in_kernel,
        grid=(depth, d // cols),
        in_specs=[pl.BlockSpec((None, d_in, cols), lambda l, i: (l, 0, i))],
        out_specs=[out(RAW_WIDTH), out(ACT_WIDTH), out(SMALL_WIDTH)],
        out_shape=[jax.ShapeDtypeStruct((depth, RAW_WIDTH, d), BF16),
                   jax.ShapeDtypeStruct((depth, ACT_WIDTH, d), BF16),
                   jax.ShapeDtypeStruct((depth, SMALL_WIDTH, d), BF16)],
        compiler_params=_params(2),
        name="w_in_layout",
    )(w_t)


def kernel(x, ffn1_norm, ffn1_w_gu, ffn1_w_down, mix_norm, w_in, fox_forget_bias, swa_sinks,
           gla_gate_w2, gla_gate_bias, gla_norm, w_branch_fox, w_branch_swa, w_branch_gla, w_out,
           ffn2_norm, ffn2_w_gu, ffn2_w_down, final_norm):
    batch, seq, d = x.shape
    depth = w_in.shape[0]
    xt = x.reshape(batch * seq, d)

    w_raw, w_act, w_small = _prep_w_in(w_in)
    gk = gla_gate_w2.shape[-1]
    w2t = jnp.zeros((depth, gk, SMALL_WIDTH), BF16)
    w2t = w2t.at[:, :, FOX_HEADS:FOX_HEADS + GLA_RANK].set(
        jnp.swapaxes(gla_gate_w2, 1, 2).astype(BF16))
    fbias_t = jnp.broadcast_to(fox_forget_bias[:, :, None], (depth, FOX_HEADS, LANES))
    gbias_t = jnp.broadcast_to(gla_gate_bias[:, :, None], (depth, gk, LANES))
    f1gu, f1d = ffn1_w_gu.astype(BF16), ffn1_w_down.astype(BF16)
    f2gu, f2d = ffn2_w_gu.astype(BF16), ffn2_w_down.astype(BF16)
    wbf = w_branch_fox.astype(BF16)
    swa_rows = w_branch_swa.reshape(depth, SWA_Q_HEADS, HEAD_DIM, d)[:, jnp.array(SWA_HEAD_ORDER)]
    wbs = swa_rows.reshape(depth, SWA_Q_HEADS * HEAD_DIM, d).astype(BF16)
    wbg, wo = w_branch_gla.astype(BF16), w_out.astype(BF16)
    fw = final_norm.reshape(1, d)

    for l in range(depth):
        xt = _ffn(xt, ffn1_norm[l].reshape(1, d), f1gu, f1d, fw, layer=l, final_norm=False)
        raw, act, small_t = _inproj(xt, mix_norm[l].reshape(1, d), w_raw, w_act, w_small, layer=l)
        c_split, bcum = _decay(small_t, w2t, fbias_t, gbias_t, batch=batch, seq=seq, layer=l)
        o_fox = _fox(raw, c_split, batch=batch, seq=seq)
        o_swa = _swa(raw, swa_sinks[l], batch=batch, seq=seq)
        o_gla = _gla(raw, bcum, act, gla_norm[l].reshape(1, -1), batch=batch, seq=seq)
        xt = _merge(xt, o_fox, o_swa, o_gla, act, wbf, wbs, wbg, wo, layer=l)
        xt = _ffn(xt, ffn2_norm[l].reshape(1, d), f2gu, f2d, fw, layer=l,
                  final_norm=(l == depth - 1))
    return xt.reshape(batch, seq, d)
```

```python
import functools

import jax
import jax.numpy as jnp
from jax import lax
from jax.experimental import pallas as pl
from jax.experimental.pallas import tpu as pltpu

F32 = jnp.float32
BF16 = jnp.bfloat16

EPS = 1e-6
NEG_INF = -1e30
HEAD_DIM = 64
FOX_HEADS = 8
SWA_Q_HEADS = 8
SWA_KV_HEADS = 2
WINDOW = 128
GLA_HEADS = 4
GLA_HEAD_K = 128
GLA_HEAD_V = 256
GLA_RANK = 16
GLA_TAU = 16.0
GLA_CHUNK = 64
GLA_SUPER = 256
FOX_Q_TILE = 1024
LOG2E = 1.4426950408889634
FOX_ONE_LANE = 3 * FOX_HEADS
FOX_Q_SCALE = HEAD_DIM ** -0.5 * LOG2E
SWA_UNROLL = 3
LANES = 128
VMEM_LIMIT_BYTES = 56 * 1024 * 1024

RAW_FQ, RAW_FK, RAW_FV = 0, 512, 1024
RAW_SQ = 1536
RAW_GQ, RAW_GK, RAW_GV = 2048, 2560, 3072
RAW_SK, RAW_SV = 4096, 4224
RAW_WIDTH = 4352
SWA_HEAD_ORDER = (0, 4, 1, 5, 2, 6, 3, 7)
ACT_WIDTH = 4096
SMALL_WIDTH = 128


def _params(n_grid_axes):
    return pltpu.CompilerParams(
        dimension_semantics=("arbitrary",) * n_grid_axes,
        vmem_limit_bytes=VMEM_LIMIT_BYTES)


def _resident(shape):
    nd = len(shape)
    return pl.BlockSpec(shape, lambda *_: (0,) * nd, pipeline_mode=pl.Buffered(1))


def _layer(shape, layer):
    nd = len(shape)
    return pl.BlockSpec((None,) + tuple(shape), lambda *_: (layer,) + (0,) * nd,
                        pipeline_mode=pl.Buffered(1))


def _rmsnorm(x, w):
    return x * lax.rsqrt(jnp.mean(x * x, axis=-1, keepdims=True) + EPS) * w


def _sigmoid(x):
    return 0.5 * jnp.tanh(0.5 * x) + 0.5


def _silu(x):
    h = 0.5 * x
    return h + h * jnp.tanh(h)


def _log2_sigmoid(x):
    return jnp.minimum(x, 0.0) * LOG2E - jnp.log2(1.0 + jnp.exp2(jnp.abs(x) * -LOG2E))


def _split3(x):
    p1 = x.astype(BF16)
    r1 = x - p1.astype(F32)
    p2 = r1.astype(BF16)
    p3 = (r1 - p2.astype(F32)).astype(BF16)
    return p1, p2, p3


def _lane_repeat(x, n):
    return jnp.concatenate([x] * n, axis=1)


def _dot(a, b):
    return jnp.dot(a, b, preferred_element_type=F32)


def _dot_nt(a, b):
    return lax.dot_general(a, b, (((1,), (1,)), ((), ())), preferred_element_type=F32)


def _dot_tn(a, b):
    return lax.dot_general(a, b, (((0,), (0,)), ((), ())), preferred_element_type=F32)


def _exact_tri_cumsum(tri, x):
    p1, p2, p3 = _split3(x)
    return _dot(tri, p1) + _dot(tri, p2) + _dot(tri, p3)


def _ffn_kernel(x_ref, nw_ref, wgu_ref, wd_ref, fw_ref, o_ref, h_ref, a_ref, *,
                f_chunk, final_norm):
    h_ref[...] = _rmsnorm(x_ref[...], nw_ref[...]).astype(BF16)
    d_ff = wd_ref.shape[0]
    for c in range(d_ff // f_chunk):
        sl = slice(c * f_chunk, (c + 1) * f_chunk)
        g = _dot(h_ref[...], wgu_ref[:, sl])
        u = _dot(h_ref[...], wgu_ref[:, d_ff + c * f_chunk:d_ff + (c + 1) * f_chunk])
        a_ref[:, sl] = (_silu(g) * u).astype(BF16)
    y = x_ref[...] + 0.5 * _dot(a_ref[...], wd_ref[...])
    if final_norm:
        y = _rmsnorm(y, fw_ref[...])
    o_ref[...] = y


def _ffn(x, nw, wgu, wd, fw, *, layer, final_norm, tm=1024, f_chunk=256):
    t, d = x.shape
    d_ff = wd.shape[1]
    row = pl.BlockSpec((tm, d), lambda i: (i, 0))
    return pl.pallas_call(
        functools.partial(_ffn_kernel, f_chunk=f_chunk, final_norm=final_norm),
        grid=(t // tm,),
        in_specs=[row, _resident((1, d)), _layer((d, 2 * d_ff), layer),
                  _layer((d_ff, d), layer), _resident((1, d))],
        out_specs=row,
        out_shape=jax.ShapeDtypeStruct((t, d), F32),
        scratch_shapes=[pltpu.VMEM((tm, d), BF16), pltpu.VMEM((tm, d_ff), BF16)],
        compiler_params=_params(1),
        name="ffn",
    )(x, nw, wgu, wd, fw)


def _inproj_kernel(x_ref, nw_ref, wraw_ref, wact_ref, wsm_ref,
                   raw_ref, act_ref, small_t_ref, h_ref, *, n_chunk, gr_width):
    h_ref[...] = _rmsnorm(x_ref[...], nw_ref[...]).astype(BF16)
    for start in range(0, raw_ref.shape[1], n_chunk):
        sl = slice(start, min(start + n_chunk, raw_ref.shape[1]))
        z = _dot_nt(h_ref[...], wraw_ref[sl, :])
        if sl.stop <= RAW_FK:
            z = z * FOX_Q_SCALE
        elif RAW_GQ <= sl.start and sl.stop <= RAW_GK:
            z = z * (GLA_HEAD_K ** -0.5)
        raw_ref[:, sl] = z.astype(BF16)
    for c in range(act_ref.shape[1] // n_chunk):
        sl = slice(c * n_chunk, (c + 1) * n_chunk)
        z = _dot_nt(h_ref[...], wact_ref[sl, :])
        if c * n_chunk < gr_width:
            act_ref[:, sl] = _silu(z).astype(BF16)
        else:
            act_ref[:, sl] = _sigmoid(z).astype(BF16)
    small_t_ref[...] = _dot_nt(wsm_ref[...], h_ref[...])


def _inproj(x, nw, w_raw, w_act, w_small, *, layer, tm=512, n_chunk=512):
    t, d = x.shape
    row = lambda w: pl.BlockSpec((tm, w), lambda i: (i, 0))
    return pl.pallas_call(
        functools.partial(_inproj_kernel, n_chunk=n_chunk, gr_width=GLA_HEADS * GLA_HEAD_V),
        grid=(t // tm,),
        in_specs=[row(d), _resident((1, d)), _layer((RAW_WIDTH, d), layer),
                  _layer((ACT_WIDTH, d), layer), _layer((SMALL_WIDTH, d), layer)],
        out_specs=[row(RAW_WIDTH), row(ACT_WIDTH),
                   pl.BlockSpec((SMALL_WIDTH, tm), lambda i: (0, i))],
        out_shape=[jax.ShapeDtypeStruct((t, RAW_WIDTH), BF16),
                   jax.ShapeDtypeStruct((t, ACT_WIDTH), BF16),
                   jax.ShapeDtypeStruct((SMALL_WIDTH, t), F32)],
        scratch_shapes=[pltpu.VMEM((tm, d), BF16)],
        compiler_params=_params(1),
        name="inproj",
    )(x, nw, w_raw, w_act, w_small)


def _decay_kernel(small_t_ref, w2t_ref, fbt_ref, gbt_ref, p_ref, bcum_ref):
    s_len = small_t_ref.shape[1]
    blk = LANES
    n_heads = FOX_HEADS
    n_blk = s_len // blk
    r = lax.broadcasted_iota(jnp.int32, (blk, 2 * blk), 0)
    cc = lax.broadcasted_iota(jnp.int32, (blk, 2 * blk), 1)
    upper_total = jnp.where((r <= cc) | (cc >= blk), 1.0, 0.0).astype(BF16)
    rr = lax.broadcasted_iota(jnp.int32, (4 * n_heads, blk), 0)
    ll = lax.broadcasted_iota(jnp.int32, (4 * n_heads, blk), 1)
    place = jnp.where(rr == ll, 1.0, 0.0).astype(BF16)
    one_row = lax.broadcasted_iota(jnp.int32, (n_heads, blk), 0) == FOX_ONE_LANE - 3 * n_heads

    sums = []
    for n in range(n_blk):
        cols = slice(n * blk, (n + 1) * blk)
        log2_f = _log2_sigmoid(small_t_ref[0:n_heads, cols] + fbt_ref[...])
        sums.append(sum(_dot(p, upper_total) for p in _split3(log2_f)))
    carry = jnp.zeros((n_heads, blk), F32)
    for n in range(n_blk):
        within = sums[n][:, :blk] + carry
        carry = carry + sums[n][:, blk:]
        c1, c2, c3 = _split3(within)
        ones = jnp.where(one_row, 1.0, 0.0).astype(BF16)
        stacked = jnp.concatenate([c1, c2, c3, ones], axis=0)
        p_ref[n * blk:(n + 1) * blk, :] = _dot_tn(stacked, place).astype(BF16)

    r = lax.broadcasted_iota(jnp.int32, (GLA_SUPER, GLA_SUPER), 0)
    cc = lax.broadcasted_iota(jnp.int32, (GLA_SUPER, GLA_SUPER), 1)
    upper = jnp.where((r <= cc) & ((r // GLA_CHUNK) == (cc // GLA_CHUNK)), 1.0, 0.0).astype(BF16)
    gate_bias = _lane_repeat(gbt_ref[...], GLA_SUPER // LANES)
    n_sup = s_len // GLA_SUPER

    def gate(sb):
        cols = slice(sb * GLA_SUPER, (sb + 1) * GLA_SUPER)
        return _dot(w2t_ref[...], small_t_ref[:, cols].astype(BF16)) + gate_bias

    def split(gate_logit):
        log2_a = _log2_sigmoid(gate_logit) * (1.0 / GLA_TAU)
        hi = log2_a.astype(BF16)
        return hi, (log2_a - hi.astype(F32)).astype(BF16)

    def cumsum(sb, parts):
        rows = slice(sb * GLA_SUPER, (sb + 1) * GLA_SUPER)
        bcum_ref[rows, :] = (_dot(parts[0], upper) + _dot(parts[1], upper)).T

    logits, parts = {}, {}
    for step in range(n_sup + 2):
        if step < n_sup:
            logits[step] = gate(step)
        if 1 <= step <= n_sup:
            parts[step - 1] = split(logits.pop(step - 1))
        if step >= 2:
            cumsum(step - 2, parts.pop(step - 2))


def _decay(small_t, w2t, fbias_t, gbias_t, *, batch, seq, layer):
    t = small_t.shape[1]
    gk = w2t.shape[1]
    return pl.pallas_call(
        _decay_kernel,
        grid=(batch,),
        in_specs=[pl.BlockSpec((SMALL_WIDTH, seq), lambda b: (0, b)),
                  _layer((gk, SMALL_WIDTH), layer), _layer((FOX_HEADS, LANES), layer),
                  _layer((gk, LANES), layer)],
        out_specs=[pl.BlockSpec((seq, LANES), lambda b: (b, 0)),
                   pl.BlockSpec((seq, gk), lambda b: (b, 0))],
        out_shape=[jax.ShapeDtypeStruct((t, LANES), BF16), jax.ShapeDtypeStruct((t, gk), F32)],
        compiler_params=_params(1),
        name="decay",
    )(small_t, w2t, fbias_t, gbias_t)


def _fox_kernel(q_ref, k_ref, v_ref, p_ref, o_ref,
                qa_ref, qb_ref, ka_ref, kb_ref, va_ref, vb_ref, m_ref, acc_ref, *, tq):
    s_len = q_ref.shape[0]
    pair = pl.program_id(1)
    n_heads = FOX_HEADS
    one_lane = FOX_ONE_LANE
    lane = lax.broadcasted_iota(jnp.int32, (1, LANES), 1)

    row = lax.broadcasted_iota(jnp.int32, (LANES, 2 * LANES), 0)
    col = lax.broadcasted_iota(jnp.int32, (LANES, 2 * LANES), 1)
    k_side = col >= LANES
    dst = col % LANES
    aug = dst % HEAD_DIM
    head = 2 * pair + jnp.where(dst < HEAD_DIM, 1, 0)
    part = aug - jnp.where(k_side, 3, 0)
    from_c = (part >= 0) & (part < 3) & (row == head + n_heads * part)
    ones_pos = aug - jnp.where(k_side, 0, 3)
    from_one = (row == one_lane) & (ones_pos >= 0) & (ones_pos < 3)
    sel = jnp.where(from_c, jnp.where(k_side, -1.0, 1.0), jnp.where(from_one, 1.0, 0.0))
    spare = _dot(p_ref[...], sel.astype(BF16))
    q_spare = spare[:, :LANES].astype(BF16)
    k_spare = spare[:, LANES:].astype(BF16)

    low = lax.broadcasted_iota(jnp.int32, (s_len, LANES), 1) < HEAD_DIM
    q, k, v = q_ref[...], k_ref[...], v_ref[...]
    qa_ref[...] = jnp.where(low, q, q_spare)
    qb_ref[...] = jnp.where(low, q_spare, q)
    ka_ref[...] = jnp.where(low, k, k_spare)
    kb_ref[...] = jnp.where(low, k_spare, k)
    va_ref[...] = jnp.where(low, v, jnp.where(lane == HEAD_DIM, 1.0, 0.0).astype(BF16))
    vb_ref[...] = jnp.where(low, jnp.where(lane == 0, 1.0, 0.0).astype(BF16), v)

    heads = ((qa_ref, ka_ref, va_ref), (qb_ref, kb_ref, vb_ref))

    def scores(h, q0, nq, k_rows, bias):
        q_ref_h, k_ref_h, _ = heads[h]
        s = _dot_nt(q_ref_h[q0:q0 + nq, :], k_ref_h[k_rows, :])
        return s if bias is None else s + bias

    def start(h, q0, nq, k_rows, bias):
        s = scores(h, q0, nq, k_rows, bias)
        m = jnp.max(s, axis=-1, keepdims=True) + jnp.zeros((nq, LANES), F32)
        p = jnp.exp2(s - _lane_repeat(m, s.shape[1] // LANES))
        acc_ref[h, q0:q0 + nq, :] = _dot(p.astype(BF16), heads[h][2][k_rows, :])
        m_ref[h, q0:q0 + nq, :] = m

    def update(h, q0, nq, k_rows):
        s = scores(h, q0, nq, k_rows, None)
        m_prev = m_ref[h, q0:q0 + nq, :]
        m_new = jnp.maximum(m_prev, jnp.max(s, axis=-1, keepdims=True))
        alpha = jnp.exp2(m_prev - m_new)
        p = jnp.exp2(s - _lane_repeat(m_new, s.shape[1] // LANES))
        acc_ref[h, q0:q0 + nq, :] = (alpha * acc_ref[h, q0:q0 + nq, :]
                                     + _dot(p.astype(BF16), heads[h][2][k_rows, :]))
        m_ref[h, q0:q0 + nq, :] = m_new

    half = tq // 2
    ri = lax.broadcasted_iota(jnp.int32, (half, tq), 0)
    ci = lax.broadcasted_iota(jnp.int32, (half, tq), 1)
    upper_bias = jnp.where(lax.broadcasted_iota(jnp.int32, (half, half), 1)
                           <= lax.broadcasted_iota(jnp.int32, (half, half), 0), 0.0, NEG_INF)
    lower_bias = jnp.where(ci <= ri + half, 0.0, NEG_INF)
    n_tiles = s_len // tq
    for qi in range(n_tiles):
        q0 = qi * tq
        for h in range(2):
            start(h, q0, half, slice(q0, q0 + half), upper_bias)
            start(h, q0 + half, half, slice(q0, q0 + tq), lower_bias)

    for qi in range(1, n_tiles):
        def kv_step(kj, _, q0=qi * tq):
            k_rows = pl.ds(pl.multiple_of(kj * tq, tq), tq)
            for h in range(2):
                update(h, q0, tq, k_rows)
            return 0

        lax.fori_loop(0, qi, kv_step, 0)

    acc_a, acc_b = acc_ref[0], acc_ref[1]
    l_a = jnp.sum(jnp.where(lane == HEAD_DIM, acc_a, 0.0), axis=-1, keepdims=True)
    l_b = jnp.sum(jnp.where(lane == 0, acc_b, 0.0), axis=-1, keepdims=True)
    o_ref[...] = jnp.where(lane < HEAD_DIM, acc_a / l_a, acc_b / l_b).astype(BF16)


def _fox(raw, c_split, *, batch, seq):
    t = raw.shape[0]
    tq = min(FOX_Q_TILE, seq)
    n_pairs = FOX_HEADS // 2
    blk = lambda off: pl.BlockSpec((seq, LANES), lambda b, j: (b, off // LANES + j))
    return pl.pallas_call(
        functools.partial(_fox_kernel, tq=tq),
        grid=(batch, n_pairs),
        in_specs=[blk(RAW_FQ), blk(RAW_FK), blk(RAW_FV),
                  pl.BlockSpec((seq, LANES), lambda b, j: (b, 0))],
        out_specs=pl.BlockSpec((seq, LANES), lambda b, j: (b, j)),
        out_shape=jax.ShapeDtypeStruct((t, FOX_HEADS * HEAD_DIM), BF16),
        scratch_shapes=[pltpu.VMEM((seq, LANES), BF16)] * 6
                       + [pltpu.VMEM((2, seq, LANES), F32)] * 2,
        compiler_params=_params(2),
        name="fox",
    )(raw, raw, raw, c_split)


def _swa_kernel(sink_ref, q_ref, k_ref, v_ref, o_ref):
    s_len = q_ref.shape[0]
    w = WINDOW
    group = SWA_Q_HEADS // SWA_KV_HEADS
    scale = HEAD_DIM ** -0.5 * LOG2E
    low = lax.broadcasted_iota(jnp.int32, (w, LANES), 1) < HEAD_DIM
    i = lax.broadcasted_iota(jnp.int32, (w, 2 * w), 0)
    j = lax.broadcasted_iota(jnp.int32, (w, 2 * w), 1)
    band = jnp.where((j > i) & (j <= i + w), 0.0, NEG_INF)
    band = jnp.concatenate([band] * group, axis=0)
    it = lax.broadcasted_iota(jnp.int32, (w, w), 0)
    jt = lax.broadcasted_iota(jnp.int32, (w, w), 1)
    tri = jnp.concatenate([jnp.where(jt <= it, 0.0, NEG_INF)] * group, axis=0)
    hrow = lax.broadcasted_iota(jnp.int32, (group * w, LANES), 0) // w

    def block(n, sinks, first):
        q_rows = pl.ds(pl.multiple_of(n * w, w), w)
        k_rows = q_rows if first else pl.ds(pl.multiple_of((n - 1) * w, w), 2 * w)
        kk = k_ref[k_rows, :]
        vv = v_ref[k_rows, :]
        q_blocks = [q_ref[q_rows, g * LANES:(g + 1) * LANES] for g in range(group)]
        outs = []
        for kvh in range(SWA_KV_HEADS):
            keep = low if kvh == 0 else jnp.logical_not(low)
            q = jnp.concatenate([jnp.where(keep, qb, jnp.zeros_like(qb)) for qb in q_blocks],
                                axis=0)
            s = _dot_nt(q, kk) * scale + (tri if first else band)
            m = jnp.maximum(jnp.max(s, axis=-1, keepdims=True), sinks[kvh])
            p = jnp.exp2(s - (m if first else _lane_repeat(m, 2 * w // LANES)))
            denom = jnp.sum(p, axis=-1, keepdims=True) + jnp.exp2(sinks[kvh] - m)
            outs.append(_dot(p.astype(BF16), vv) / denom)
        for g in range(group):
            rows = slice(g * w, (g + 1) * w)
            o_ref[q_rows, g * LANES:(g + 1) * LANES] = jnp.where(
                low, outs[0][rows, :], outs[1][rows, :]).astype(BF16)

    sinks = []
    for kvh in range(SWA_KV_HEADS):
        sink = jnp.zeros((group * w, LANES), F32)
        for g in range(group):
            sink = jnp.where(hrow == g, sink_ref[kvh * group + g] * LOG2E, sink)
        sinks.append(sink)
    block(0, sinks, True)

    def body(n, _):
        block(n, sinks, False)
        return 0

    lax.fori_loop(1, s_len // w, body, 0, unroll=SWA_UNROLL)


def _swa(raw, sinks, *, batch, seq):
    t = raw.shape[0]
    qw = SWA_Q_HEADS * HEAD_DIM
    kw = SWA_KV_HEADS * HEAD_DIM
    return pl.pallas_call(
        _swa_kernel,
        grid=(batch,),
        in_specs=[pl.BlockSpec(memory_space=pltpu.SMEM),
                  pl.BlockSpec((seq, qw), lambda b: (b, RAW_SQ // qw)),
                  pl.BlockSpec((seq, kw), lambda b: (b, RAW_SK // kw)),
                  pl.BlockSpec((seq, kw), lambda b: (b, RAW_SV // kw))],
        out_specs=pl.BlockSpec((seq, qw), lambda b: (b, 0)),
        out_shape=jax.ShapeDtypeStruct((t, qw), BF16),
        compiler_params=_params(1),
        name="swa",
    )(sinks, raw, raw, raw)


def _gla_kernel(q_ref, k_ref, v_ref, b_ref, gr_ref, nw_ref, o_ref, st_ref, *, n_heads, unroll):
    s_len = q_ref.shape[0]
    ck, sup = GLA_CHUNK, GLA_SUPER
    dk, dv = GLA_HEAD_K, GLA_HEAD_V
    per = sup // ck
    r = lax.broadcasted_iota(jnp.int32, (sup, sup), 0)
    c = lax.broadcasted_iota(jnp.int32, (sup, sup), 1)
    intra = (r >= c) & ((r // ck) == (c // ck))
    st_ref[...] = jnp.zeros(st_ref.shape, F32)

    def head_block(rows, h):
        kcols = slice(h * dk, (h + 1) * dk)
        vcols = slice(h * dv, (h + 1) * dv)
        b = b_ref[rows, kcols]
        b3 = b.reshape(per, ck, dk)
        b_last = b3[:, ck - 1:ck, :]
        qf = q_ref[rows, kcols].astype(F32)
        kf = k_ref[rows, kcols].astype(F32)
        q_t = (qf * jnp.exp2(b)).astype(BF16)
        k_t = (kf * jnp.exp2(-b)).astype(BF16)
        k_s = (kf.reshape(per, ck, dk) * jnp.exp2(b_last - b3)).astype(BF16)
        decay = jnp.exp2(b_last)
        v = v_ref[rows, vcols]
        a = jnp.where(intra, _dot_nt(q_t, k_t), 0.0)
        o_intra = _dot(a.astype(BF16), v)
        st = st_ref[h]
        outs = []
        for cidx in range(per):
            chunk = slice(cidx * ck, (cidx + 1) * ck)
            outs.append(o_intra[chunk, :] + _dot_nt(q_t[chunk, :], st.astype(BF16)))
            st = st * decay[cidx] + _dot_tn(v[chunk, :], k_s[cidx])
        st_ref[h] = st
        o = jnp.concatenate(outs, axis=0)
        o = o * lax.rsqrt(jnp.mean(o * o, axis=-1, keepdims=True) + EPS) * nw_ref[...]
        o_ref[rows, vcols] = (o * gr_ref[rows, vcols].astype(F32)).astype(BF16)

    def super_block(sb, _):
        rows = pl.ds(pl.multiple_of(sb * sup, sup), sup)
        for h in range(n_heads):
            head_block(rows, h)
        return 0

    lax.fori_loop(0, s_len // sup, super_block, 0, unroll=unroll)


def _gla(raw, bcum, act, gla_norm, *, batch, seq, heads_per_step=2, unroll=2):
    t = raw.shape[0]
    dk, dv = GLA_HEAD_K * heads_per_step, GLA_HEAD_V * heads_per_step
    assert RAW_GQ % dk == 0 and RAW_GK % dk == 0 and RAW_GV % dv == 0
    return pl.pallas_call(
        functools.partial(_gla_kernel, n_heads=heads_per_step, unroll=unroll),
        grid=(batch, GLA_HEADS // heads_per_step),
        in_specs=[pl.BlockSpec((seq, dk), lambda b, h: (b, RAW_GQ // dk + h)),
                  pl.BlockSpec((seq, dk), lambda b, h: (b, RAW_GK // dk + h)),
                  pl.BlockSpec((seq, dv), lambda b, h: (b, RAW_GV // dv + h)),
                  pl.BlockSpec((seq, dk), lambda b, h: (b, h)),
                  pl.BlockSpec((seq, dv), lambda b, h: (b, h)),
                  pl.BlockSpec((1, GLA_HEAD_V), lambda b, h: (0, 0))],
        out_specs=pl.BlockSpec((seq, dv), lambda b, h: (b, h)),
        out_shape=jax.ShapeDtypeStruct((t, GLA_HEADS * GLA_HEAD_V), BF16),
        scratch_shapes=[pltpu.VMEM((heads_per_step, GLA_HEAD_V, GLA_HEAD_K), F32)],
        compiler_params=_params(2),
        name="gla",
    )(raw, raw, raw, bcum, act, gla_norm)


def _merge_kernel(x_ref, of_ref, os_ref, og_ref, g0_ref, g1_ref, g2_ref,
                  wf_ref, ws_ref, wg_ref, wo_ref, o_ref):
    merged = (g0_ref[...].astype(F32) * _dot(of_ref[...], wf_ref[...])
              + g1_ref[...].astype(F32) * _dot(os_ref[...], ws_ref[...])
              + g2_ref[...].astype(F32) * _dot(og_ref[...], wg_ref[...]))
    o_ref[...] = x_ref[...] + _dot(merged.astype(BF16), wo_ref[...])


def _merge(x, o_fox, o_swa, o_gla, act, wf, ws, wg, wo, *, layer, tm=1024):
    t, d = x.shape
    row = lambda w, j=0: pl.BlockSpec((tm, w), lambda i: (i, j))
    return pl.pallas_call(
        _merge_kernel,
        grid=(t // tm,),
        in_specs=[row(d), row(o_fox.shape[1]), row(o_swa.shape[1]), row(o_gla.shape[1]),
                  row(d, 1), row(d, 2), row(d, 3),
                  _layer(wf.shape[1:], layer), _layer(ws.shape[1:], layer),
                  _layer(wg.shape[1:], layer), _layer(wo.shape[1:], layer)],
        out_specs=row(d),
        out_shape=jax.ShapeDtypeStruct((t, d), F32),
        compiler_params=_params(1),
        name="merge",
    )(x, o_fox, o_swa, o_gla, act, act, act, wf, ws, wg, wo)


def _w_in_pieces():
    sizes = (512, 512, 512, FOX_HEADS, 512, 128, 128, 512, 512, 1024, GLA_RANK, 1024, 3072)
    offs = [0]
    for s in sizes:
        offs.append(offs[-1] + s)
    fq, fk, fv, ff, sq, sk, sv, gq, gk, gv, glr, gr, gates = offs[:-1]
    h = HEAD_DIM
    raw = ([(fq, 512), (fk, 512), (fv, 512)] + [(sq + h * i, h) for i in SWA_HEAD_ORDER]
           + [(gq, 512), (gk, 512), (gv, 1024), (sk, 2 * h), (sv, 2 * h)])
    moves, dst = [], 0
    for src, width in raw:
        moves.append(("raw", dst, src, width))
        dst += width
    assert dst == RAW_WIDTH
    moves.append(("act", 0, gr, ACT_WIDTH))
    moves += [("small", 0, ff, FOX_HEADS), ("small", FOX_HEADS, glr, GLA_RANK)]
    return moves


def _w_in_kernel(w_ref, raw_ref, act_ref, small_ref):
    outs = {"raw": raw_ref, "act": act_ref, "small": small_ref}
    small_ref[...] = jnp.zeros(small_ref.shape, BF16)
    for name, dst, src, width in _w_in_pieces():
        outs[name][dst:dst + width, :] = w_ref[src:src + width, :].astype(BF16)


def _prep_w_in(w_in, *, cols=256):
    depth, d, d_in = w_in.shape
    w_t = jnp.swapaxes(w_in, 1, 2)
    out = lambda n: pl.BlockSpec((None, n, cols), lambda l, i: (l, 0, i))
    return pl.pallas_call(
        _w_in_kernel,
        grid=(depth, d // cols),
        in_specs=[pl.BlockSpec((None, d_in, cols), lambda l, i: (l, 0, i))],
        out_specs=[out(RAW_WIDTH), out(ACT_WIDTH), out(SMALL_WIDTH)],
        out_shape=[jax.ShapeDtypeStruct((depth, RAW_WIDTH, d), BF16),
                   jax.ShapeDtypeStruct((depth, ACT_WIDTH, d), BF16),
                   jax.ShapeDtypeStruct((depth, SMALL_WIDTH, d), BF16)],
        compiler_params=_params(2),
        name="w_in_layout",
    )(w_t)


def kernel(x, ffn1_norm, ffn1_w_gu, ffn1_w_down, mix_norm, w_in, fox_forget_bias, swa_sinks,
           gla_gate_w2, gla_gate_bias, gla_norm, w_branch_fox, w_branch_swa, w_branch_gla, w_out,
           ffn2_norm, ffn2_w_gu, ffn2_w_down, final_norm):
    batch, seq, d = x.shape
    depth = w_in.shape[0]
    xt = x.reshape(batch * seq, d)

    w_raw, w_act, w_small = _prep_w_in(w_in)
    gk = gla_gate_w2.shape[-1]
    w2t = jnp.zeros((depth, gk, SMALL_WIDTH), BF16)
    w2t = w2t.at[:, :, FOX_HEADS:FOX_HEADS + GLA_RANK].set(
        jnp.swapaxes(gla_gate_w2, 1, 2).astype(BF16))
    fbias_t = jnp.broadcast_to(fox_forget_bias[:, :, None], (depth, FOX_HEADS, LANES))
    gbias_t = jnp.broadcast_to(gla_gate_bias[:, :, None], (depth, gk, LANES))
    f1gu, f1d = ffn1_w_gu.astype(BF16), ffn1_w_down.astype(BF16)
    f2gu, f2d = ffn2_w_gu.astype(BF16), ffn2_w_down.astype(BF16)
    wbf = w_branch_fox.astype(BF16)
    swa_rows = w_branch_swa.reshape(depth, SWA_Q_HEADS, HEAD_DIM, d)[:, jnp.array(SWA_HEAD_ORDER)]
    wbs = swa_rows.reshape(depth, SWA_Q_HEADS * HEAD_DIM, d).astype(BF16)
    wbg, wo = w_branch_gla.astype(BF16), w_out.astype(BF16)
    fw = final_norm.reshape(1, d)

    for l in range(depth):
        xt = _ffn(xt, ffn1_norm[l].reshape(1, d), f1gu, f1d, fw, layer=l, final_norm=False)
        raw, act, small_t = _inproj(xt, mix_norm[l].reshape(1, d), w_raw, w_act, w_small, layer=l)
        c_split, bcum = _decay(small_t, w2t, fbias_t, gbias_t, batch=batch, seq=seq, layer=l)
        o_fox = _fox(raw, c_split, batch=batch, seq=seq)
        o_swa = _swa(raw, swa_sinks[l], batch=batch, seq=seq)
        o_gla = _gla(raw, bcum, act, gla_norm[l].reshape(1, -1), batch=batch, seq=seq)
        xt = _merge(xt, o_fox, o_swa, o_gla, act, wbf, wbs, wbg, wo, layer=l)
        xt = _ffn(xt, ffn2_norm[l].reshape(1, d), f2gu, f2d, fw, layer=l,
                  final_norm=(l == depth - 1))
    return xt.reshape(batch, seq, d)
```

```python
import functools

import jax
import jax.numpy as jnp
from jax import lax
from jax.experimental import pallas as pl
from jax.experimental.pallas import tpu as pltpu

F32 = jnp.float32
BF16 = jnp.bfloat16

EPS = 1e-6
NEG_INF = -1e30
HEAD_DIM = 64
FOX_HEADS = 8
SWA_Q_HEADS = 8
SWA_KV_HEADS = 2
WINDOW = 128
GLA_HEADS = 4
GLA_HEAD_K = 128
GLA_HEAD_V = 256
GLA_RANK = 16
GLA_TAU = 16.0
GLA_CHUNK = 64
GLA_SUPER = 256
FOX_Q_TILE = 1024
LOG2E = 1.4426950408889634
FOX_ONE_LANE = 3 * FOX_HEADS
FOX_Q_SCALE = HEAD_DIM ** -0.5 * LOG2E
SWA_UNROLL = 3
LANES = 128
VMEM_LIMIT_BYTES = 56 * 1024 * 1024

RAW_FQ, RAW_FK, RAW_FV = 0, 512, 1024
RAW_SQ = 1536
RAW_GQ, RAW_GK, RAW_GV = 2048, 2560, 3072
RAW_SK, RAW_SV = 4096, 4224
RAW_WIDTH = 4352
SWA_HEAD_ORDER = (0, 4, 1, 5, 2, 6, 3, 7)
ACT_WIDTH = 4096
SMALL_WIDTH = 128


def _params(n_grid_axes):
    return pltpu.CompilerParams(
        dimension_semantics=("arbitrary",) * n_grid_axes,
        vmem_limit_bytes=VMEM_LIMIT_BYTES)


def _resident(shape):
    nd = len(shape)
    return pl.BlockSpec(shape, lambda *_: (0,) * nd, pipeline_mode=pl.Buffered(1))


def _layer(shape, layer):
    nd = len(shape)
    return pl.BlockSpec((None,) + tuple(shape), lambda *_: (layer,) + (0,) * nd,
                        pipeline_mode=pl.Buffered(1))


def _rmsnorm(x, w):
    return x * lax.rsqrt(jnp.mean(x * x, axis=-1, keepdims=True) + EPS) * w


def _sigmoid(x):
    return 0.5 * jnp.tanh(0.5 * x) + 0.5


def _silu(x):
    h = 0.5 * x
    return h + h * jnp.tanh(h)


def _log2_sigmoid(x):
    return jnp.minimum(x, 0.0) * LOG2E - jnp.log2(1.0 + jnp.exp2(jnp.abs(x) * -LOG2E))


def _split3(x):
    p1 = x.astype(BF16)
    r1 = x - p1.astype(F32)
    p2 = r1.astype(BF16)
    p3 = (r1 - p2.astype(F32)).astype(BF16)
    return p1, p2, p3


def _lane_repeat(x, n):
    return jnp.concatenate([x] * n, axis=1)


def _dot(a, b):
    return jnp.dot(a, b, preferred_element_type=F32)


def _dot_nt(a, b):
    return lax.dot_general(a, b, (((1,), (1,)), ((), ())), preferred_element_type=F32)


def _dot_tn(a, b):
    return lax.dot_general(a, b, (((0,), (0,)), ((), ())), preferred_element_type=F32)


def _exact_tri_cumsum(tri, x):
    p1, p2, p3 = _split3(x)
    return _dot(tri, p1) + _dot(tri, p2) + _dot(tri, p3)


def _ffn_kernel(x_ref, nw_ref, wgu_ref, wd_ref, fw_ref, o_ref, h_ref, a_ref, *,
                f_chunk, final_norm):
    h_ref[...] = _rmsnorm(x_ref[...], nw_ref[...]).astype(BF16)
    d_ff = wd_ref.shape[0]
    for c in range(d_ff // f_chunk):
        sl = slice(c * f_chunk, (c + 1) * f_chunk)
        g = _dot(h_ref[...], wgu_ref[:, sl])
        u = _dot(h_ref[...], wgu_ref[:, d_ff + c * f_chunk:d_ff + (c + 1) * f_chunk])
        a_ref[:, sl] = (_silu(g) * u).astype(BF16)
    y = x_ref[...] + 0.5 * _dot(a_ref[...], wd_ref[...])
    if final_norm:
        y = _rmsnorm(y, fw_ref[...])
    o_ref[...] = y


def _ffn(x, nw, wgu, wd, fw, *, layer, final_norm, tm=1024, f_chunk=256):
    t, d = x.shape
    d_ff = wd.shape[1]
    row = pl.BlockSpec((tm, d), lambda i: (i, 0))
    return pl.pallas_call(
        functools.partial(_ffn_kernel, f_chunk=f_chunk, final_norm=final_norm),
        grid=(t // tm,),
        in_specs=[row, _resident((1, d)), _layer((d, 2 * d_ff), layer),
                  _layer((d_ff, d), layer), _resident((1, d))],
        out_specs=row,
        out_shape=jax.ShapeDtypeStruct((t, d), F32),
        scratch_shapes=[pltpu.VMEM((tm, d), BF16), pltpu.VMEM((tm, d_ff), BF16)],
        compiler_params=_params(1),
        name="ffn",
    )(x, nw, wgu, wd, fw)


def _inproj_kernel(x_ref, nw_ref, wraw_ref, wact_ref, wsm_ref,
                   raw_ref, act_ref, small_t_ref, h_ref, *, n_chunk, gr_width):
    h_ref[...] = _rmsnorm(x_ref[...], nw_ref[...]).astype(BF16)
    for start in range(0, raw_ref.shape[1], n_chunk):
        sl = slice(start, min(start + n_chunk, raw_ref.shape[1]))
        z = _dot_nt(h_ref[...], wraw_ref[sl, :])
        if sl.stop <= RAW_FK:
            z = z * FOX_Q_SCALE
        elif RAW_GQ <= sl.start and sl.stop <= RAW_GK:
            z = z * (GLA_HEAD_K ** -0.5)
        raw_ref[:, sl] = z.astype(BF16)
    for c in range(act_ref.shape[1] // n_chunk):
        sl = slice(c * n_chunk, (c + 1) * n_chunk)
        z = _dot_nt(h_ref[...], wact_ref[sl, :])
        if c * n_chunk < gr_width:
            act_ref[:, sl] = _silu(z).astype(BF16)
        else:
            act_ref[:, sl] = _sigmoid(z).astype(BF16)
    small_t_ref[...] = _dot_nt(wsm_ref[...], h_ref[...])


def _inproj(x, nw, w_raw, w_act, w_small, *, layer, tm=512, n_chunk=512):
    t, d = x.shape
    row = lambda w: pl.BlockSpec((tm, w), lambda i: (i, 0))
    return pl.pallas_call(
        functools.partial(_inproj_kernel, n_chunk=n_chunk, gr_width=GLA_HEADS * GLA_HEAD_V),
        grid=(t // tm,),
        in_specs=[row(d), _resident((1, d)), _layer((RAW_WIDTH, d), layer),
                  _layer((ACT_WIDTH, d), layer), _layer((SMALL_WIDTH, d), layer)],
        out_specs=[row(RAW_WIDTH), row(ACT_WIDTH),
                   pl.BlockSpec((SMALL_WIDTH, tm), lambda i: (0, i))],
        out_shape=[jax.ShapeDtypeStruct((t, RAW_WIDTH), BF16),
                   jax.ShapeDtypeStruct((t, ACT_WIDTH), BF16),
                   jax.ShapeDtypeStruct((SMALL_WIDTH, t), F32)],
        scratch_shapes=[pltpu.VMEM((tm, d), BF16)],
        compiler_params=_params(1),
        name="inproj",
    )(x, nw, w_raw, w_act, w_small)


def _decay_kernel(small_t_ref, w2t_ref, fbt_ref, gbt_ref, p_ref, bcum_ref):
    s_len = small_t_ref.shape[1]
    blk = LANES
    n_heads = FOX_HEADS
    n_blk = s_len // blk
    r = lax.broadcasted_iota(jnp.int32, (blk, 2 * blk), 0)
    cc = lax.broadcasted_iota(jnp.int32, (blk, 2 * blk), 1)
    upper_total = jnp.where((r <= cc) | (cc >= blk), 1.0, 0.0).astype(BF16)
    rr = lax.broadcasted_iota(jnp.int32, (4 * n_heads, blk), 0)
    ll = lax.broadcasted_iota(jnp.int32, (4 * n_heads, blk), 1)
    place = jnp.where(rr == ll, 1.0, 0.0).astype(BF16)
    one_row = lax.broadcasted_iota(jnp.int32, (n_heads, blk), 0) == FOX_ONE_LANE - 3 * n_heads

    sums = []
    for n in range(n_blk):
        cols = slice(n * blk, (n + 1) * blk)
        log2_f = _log2_sigmoid(small_t_ref[0:n_heads, cols] + fbt_ref[...])
        sums.append(sum(_dot(p, upper_total) for p in _split3(log2_f)))
    carry = jnp.zeros((n_heads, blk), F32)
    for n in range(n_blk):
        within = sums[n][:, :blk] + carry
        carry = carry + sums[n][:, blk:]
        c1, c2, c3 = _split3(within)
        ones = jnp.where(one_row, 1.0, 0.0).astype(BF16)
        stacked = jnp.concatenate([c1, c2, c3, ones], axis=0)
        p_ref[n * blk:(n + 1) * blk, :] = _dot_tn(stacked, place).astype(BF16)

    r = lax.broadcasted_iota(jnp.int32, (GLA_SUPER, GLA_SUPER), 0)
    cc = lax.broadcasted_iota(jnp.int32, (GLA_SUPER, GLA_SUPER), 1)
    upper = jnp.where((r <= cc) & ((r // GLA_CHUNK) == (cc // GLA_CHUNK)), 1.0, 0.0).astype(BF16)
    gate_bias = _lane_repeat(gbt_ref[...], GLA_SUPER // LANES)
    n_sup = s_len // GLA_SUPER

    def gate(sb):
        cols = slice(sb * GLA_SUPER, (sb + 1) * GLA_SUPER)
        return _dot(w2t_ref[...], small_t_ref[:, cols].astype(BF16)) + gate_bias

    def split(gate_logit):
        log2_a = _log2_sigmoid(gate_logit) * (1.0 / GLA_TAU)
        hi = log2_a.astype(BF16)
        return hi, (log2_a - hi.astype(F32)).astype(BF16)

    def cumsum(sb, parts):
        rows = slice(sb * GLA_SUPER, (sb + 1) * GLA_SUPER)
        bcum_ref[rows, :] = (_dot(parts[0], upper) + _dot(parts[1], upper)).T

    logits, parts = {}, {}
    for step in range(n_sup + 2):
        if step < n_sup:
            logits[step] = gate(step)
        if 1 <= step <= n_sup:
            parts[step - 1] = split(logits.pop(step - 1))
        if step >= 2:
            cumsum(step - 2, parts.pop(step - 2))


def _decay(small_t, w2t, fbias_t, gbias_t, *, batch, seq, layer):
    t = small_t.shape[1]
    gk = w2t.shape[1]
    return pl.pallas_call(
        _decay_kernel,
        grid=(batch,),
        in_specs=[pl.BlockSpec((SMALL_WIDTH, seq), lambda b: (0, b)),
                  _layer((gk, SMALL_WIDTH), layer), _layer((FOX_HEADS, LANES), layer),
                  _layer((gk, LANES), layer)],
        out_specs=[pl.BlockSpec((seq, LANES), lambda b: (b, 0)),
                   pl.BlockSpec((seq, gk), lambda b: (b, 0))],
        out_shape=[jax.ShapeDtypeStruct((t, LANES), BF16), jax.ShapeDtypeStruct((t, gk), F32)],
        compiler_params=_params(1),
        name="decay",
    )(small_t, w2t, fbias_t, gbias_t)


def _fox_kernel(q_ref, k_ref, v_ref, p_ref, o_ref,
                qa_ref, qb_ref, ka_ref, kb_ref, va_ref, vb_ref, m_ref, acc_ref, *, tq):
    s_len = q_ref.shape[0]
    pair = pl.program_id(1)
    n_heads = FOX_HEADS
    one_lane = FOX_ONE_LANE
    lane = lax.broadcasted_iota(jnp.int32, (1, LANES), 1)

    row = lax.broadcasted_iota(jnp.int32, (LANES, 2 * LANES), 0)
    col = lax.broadcasted_iota(jnp.int32, (LANES, 2 * LANES), 1)
    k_side = col >= LANES
    dst = col % LANES
    aug = dst % HEAD_DIM
    head = 2 * pair + jnp.where(dst < HEAD_DIM, 1, 0)
    part = aug - jnp.where(k_side, 3, 0)
    from_c = (part >= 0) & (part < 3) & (row == head + n_heads * part)
    ones_pos = aug - jnp.where(k_side, 0, 3)
    from_one = (row == one_lane) & (ones_pos >= 0) & (ones_pos < 3)
    sel = jnp.where(from_c, jnp.where(k_side, -1.0, 1.0), jnp.where(from_one, 1.0, 0.0))
    spare = _dot(p_ref[...], sel.astype(BF16))
    q_spare = spare[:, :LANES].astype(BF16)
    k_spare = spare[:, LANES:].astype(BF16)

    low = lax.broadcasted_iota(jnp.int32, (s_len, LANES), 1) < HEAD_DIM
    q, k, v = q_ref[...], k_ref[...], v_ref[...]
    qa_ref[...] = jnp.where(low, q, q_spare)
    qb_ref[...] = jnp.where(low, q_spare, q)
    ka_ref[...] = jnp.where(low, k, k_spare)
    kb_ref[...] = jnp.where(low, k_spare, k)
    va_ref[...] = jnp.where(low, v, jnp.where(lane == HEAD_DIM, 1.0, 0.0).astype(BF16))
    vb_ref[...] = jnp.where(low, jnp.where(lane == 0, 1.0, 0.0).astype(BF16), v)

    heads = ((qa_ref, ka_ref, va_ref), (qb_ref, kb_ref, vb_ref))

    def scores(h, q0, nq, k_rows, bias):
        q_ref_h, k_ref_h, _ = heads[h]
        s = _dot_nt(q_ref_h[q0:q0 + nq, :], k_ref_h[k_rows, :])
        return s if bias is None else s + bias

    def start(h, q0, nq, k_rows, bias):
        s = scores(h, q0, nq, k_rows, bias)
        m = jnp.max(s, axis=-1, keepdims=True) + jnp.zeros((nq, LANES), F32)
        p = jnp.exp2(s - _lane_repeat(m, s.shape[1] // LANES))
        acc_ref[h, q0:q0 + nq, :] = _dot(p.astype(BF16), heads[h][2][k_rows, :])
        m_ref[h, q0:q0 + nq, :] = m

    def update(h, q0, nq, k_rows):
        s = scores(h, q0, nq, k_rows, None)
        m_prev = m_ref[h, q0:q0 + nq, :]
        m_new = jnp.maximum(m_prev, jnp.max(s, axis=-1, keepdims=True))
        alpha = jnp.exp2(m_prev - m_new)
        p = jnp.exp2(s - _lane_repeat(m_new, s.shape[1] // LANES))
        acc_ref[h, q0:q0 + nq, :] = (alpha * acc_ref[h, q0:q0 + nq, :]
                                     + _dot(p.astype(BF16), heads[h][2][k_rows, :]))
        m_ref[h, q0:q0 + nq, :] = m_new

    half = tq // 2
    ri = lax.broadcasted_iota(jnp.int32, (half, tq), 0)
    ci = lax.broadcasted_iota(jnp.int32, (half, tq), 1)
    upper_bias = jnp.where(lax.broadcasted_iota(jnp.int32, (half, half), 1)
                           <= lax.broadcasted_iota(jnp.int32, (half, half), 0), 0.0, NEG_INF)
    lower_bias = jnp.where(ci <= ri + half, 0.0, NEG_INF)
    n_tiles = s_len // tq
    for qi in range(n_tiles):
        q0 = qi * tq
        for h in range(2):
            start(h, q0, half, slice(q0, q0 + half), upper_bias)
            start(h, q0 + half, half, slice(q0, q0 + tq), lower_bias)

    for qi in range(1, n_tiles):
        def kv_step(kj, _, q0=qi * tq):
            k_rows = pl.ds(pl.multiple_of(kj * tq, tq), tq)
            for h in range(2):
                update(h, q0, tq, k_rows)
            return 0

        lax.fori_loop(0, qi, kv_step, 0)

    acc_a, acc_b = acc_ref[0], acc_ref[1]
    l_a = jnp.sum(jnp.where(lane == HEAD_DIM, acc_a, 0.0), axis=-1, keepdims=True)
    l_b = jnp.sum(jnp.where(lane == 0, acc_b, 0.0), axis=-1, keepdims=True)
    o_ref[...] = jnp.where(lane < HEAD_DIM, acc_a / l_a, acc_b / l_b).astype(BF16)


def _fox(raw, c_split, *, batch, seq):
    t = raw.shape[0]
    tq = min(FOX_Q_TILE, seq)
    n_pairs = FOX_HEADS // 2
    blk = lambda off: pl.BlockSpec((seq, LANES), lambda b, j: (b, off // LANES + j))
    return pl.pallas_call(
        functools.partial(_fox_kernel, tq=tq),
        grid=(batch, n_pairs),
        in_specs=[blk(RAW_FQ), blk(RAW_FK), blk(RAW_FV),
                  pl.BlockSpec((seq, LANES), lambda b, j: (b, 0))],
        out_specs=pl.BlockSpec((seq, LANES), lambda b, j: (b, j)),
        out_shape=jax.ShapeDtypeStruct((t, FOX_HEADS * HEAD_DIM), BF16),
        scratch_shapes=[pltpu.VMEM((seq, LANES), BF16)] * 6
                       + [pltpu.VMEM((2, seq, LANES), F32)] * 2,
        compiler_params=_params(2),
        name="fox",
    )(raw, raw, raw, c_split)


def _swa_kernel(sink_ref, q_ref, k_ref, v_ref, o_ref):
    s_len = q_ref.shape[0]
    w = WINDOW
    group = SWA_Q_HEADS // SWA_KV_HEADS
    scale = HEAD_DIM ** -0.5 * LOG2E
    low = lax.broadcasted_iota(jnp.int32, (w, LANES), 1) < HEAD_DIM
    i = lax.broadcasted_iota(jnp.int32, (w, 2 * w), 0)
    j = lax.broadcasted_iota(jnp.int32, (w, 2 * w), 1)
    band = jnp.where((j > i) & (j <= i + w), 0.0, NEG_INF)
    band = jnp.concatenate([band] * group, axis=0)
    it = lax.broadcasted_iota(jnp.int32, (w, w), 0)
    jt = lax.broadcasted_iota(jnp.int32, (w, w), 1)
    tri = jnp.concatenate([jnp.where(jt <= it, 0.0, NEG_INF)] * group, axis=0)
    hrow = lax.broadcasted_iota(jnp.int32, (group * w, LANES), 0) // w

    def block(n, sinks, first):
        q_rows = pl.ds(pl.multiple_of(n * w, w), w)
        k_rows = q_rows if first else pl.ds(pl.multiple_of((n - 1) * w, w), 2 * w)
        kk = k_ref[k_rows, :]
        vv = v_ref[k_rows, :]
        q_blocks = [q_ref[q_rows, g * LANES:(g + 1) * LANES] for g in range(group)]
        outs = []
        for kvh in range(SWA_KV_HEADS):
            keep = low if kvh == 0 else jnp.logical_not(low)
            q = jnp.concatenate([jnp.where(keep, qb, jnp.zeros_like(qb)) for qb in q_blocks],
                                axis=0)
            s = _dot_nt(q, kk) * scale + (tri if first else band)
            m = jnp.maximum(jnp.max(s, axis=-1, keepdims=True), sinks[kvh])
            p = jnp.exp2(s - (m if first else _lane_repeat(m, 2 * w // LANES)))
            denom = jnp.sum(p, axis=-1, keepdims=True) + jnp.exp2(sinks[kvh] - m)
            outs.append(_dot(p.astype(BF16), vv) / denom)
        for g in range(group):
            rows = slice(g * w, (g + 1) * w)
            o_ref[q_rows, g * LANES:(g + 1) * LANES] = jnp.where(
                low, outs[0][rows, :], outs[1][rows, :]).astype(BF16)

    sinks = []
    for kvh in range(SWA_KV_HEADS):
        sink = jnp.zeros((group * w, LANES), F32)
        for g in range(group):
            sink = jnp.where(hrow == g, sink_ref[kvh * group + g] * LOG2E, sink)
        sinks.append(sink)
    block(0, sinks, True)

    def body(n, _):
        block(n, sinks, False)
        return 0

    lax.fori_loop(1, s_len // w, body, 0, unroll=SWA_UNROLL)


def _swa(raw, sinks, *, batch, seq):
    t = raw.shape[0]
    qw = SWA_Q_HEADS * HEAD_DIM
    kw = SWA_KV_HEADS * HEAD_DIM
    return pl.pallas_call(
        _swa_kernel,
        grid=(batch,),
        in_specs=[pl.BlockSpec(memory_space=pltpu.SMEM),
                  pl.BlockSpec((seq, qw), lambda b: (b, RAW_SQ // qw)),
                  pl.BlockSpec((seq, kw), lambda b: (b, RAW_SK // kw)),
                  pl.BlockSpec((seq, kw), lambda b: (b, RAW_SV // kw))],
        out_specs=pl.BlockSpec((seq, qw), lambda b: (b, 0)),
        out_shape=jax.ShapeDtypeStruct((t, qw), BF16),
        compiler_params=_params(1),
        name="swa",
    )(sinks, raw, raw, raw)


def _gla_kernel(q_ref, k_ref, v_ref, b_ref, gr_ref, nw_ref, o_ref, st_ref, *, n_heads, unroll):
    s_len = q_ref.shape[0]
    ck, sup = GLA_CHUNK, GLA_SUPER
    dk, dv = GLA_HEAD_K, GLA_HEAD_V
    per = sup // ck
    r = lax.broadcasted_iota(jnp.int32, (sup, sup), 0)
    c = lax.broadcasted_iota(jnp.int32, (sup, sup), 1)
    intra = (r >= c) & ((r // ck) == (c // ck))
    st_ref[...] = jnp.zeros(st_ref.shape, F32)

    def decayed(rows, h):
        kcols = slice(h * dk, (h + 1) * dk)
        b = b_ref[rows, kcols]
        b3 = b.reshape(per, ck, dk)
        b_last = b3[:, ck - 1:ck, :]
        qf = q_ref[rows, kcols].astype(F32)
        kf = k_ref[rows, kcols].astype(F32)
        q_t = (qf * jnp.exp2(b)).astype(BF16)
        k_t = (kf * jnp.exp2(-b)).astype(BF16)
        k_s = (kf.reshape(per, ck, dk) * jnp.exp2(b_last - b3)).astype(BF16)
        return q_t, k_t, k_s, jnp.exp2(b_last), v_ref[rows, h * dv:(h + 1) * dv]

    def products(q_t, k_t, k_s, decay, v):
        qk = _dot_nt(q_t, k_t)
        kv = [_dot_tn(v[c * ck:(c + 1) * ck, :], k_s[c]) for c in range(per)]
        return qk, kv

    def intra_chunk(qk, v):
        return _dot(jnp.where(intra, qk, 0.0).astype(BF16), v)

    def carry_state(h, q_t, decay, kv, o_intra):
        st = st_ref[h]
        outs = []
        for c in range(per):
            chunk = slice(c * ck, (c + 1) * ck)
            outs.append(o_intra[chunk, :] + _dot_nt(q_t[chunk, :], st.astype(BF16)))
            st = st * decay[c] + kv[c]
        st_ref[h] = st
        return jnp.concatenate(outs, axis=0)

    def finish(rows, h, o):
        vcols = slice(h * dv, (h + 1) * dv)
        o = o * lax.rsqrt(jnp.mean(o * o, axis=-1, keepdims=True) + EPS) * nw_ref[...]
        o_ref[rows, vcols] = (o * gr_ref[rows, vcols].astype(F32)).astype(BF16)

    def step(i, _):
        work = []
        for u in range(unroll):
            start = pl.multiple_of((i * unroll + u) * sup, sup)
            work += [(pl.ds(start, sup), h) for h in range(n_heads)]
        pre = [decayed(rows, h) for rows, h in work]
        prod = [products(*p) for p in pre]
        o_intra = [intra_chunk(qk, p[4]) for (qk, _), p in zip(prod, pre)]
        outs = [carry_state(h, p[0], p[3], kv, oi)
                for (_, h), p, (_, kv), oi in zip(work, pre, prod, o_intra)]
        for (rows, h), o in zip(work, outs):
            finish(rows, h, o)
        return 0

    lax.fori_loop(0, s_len // (sup * unroll), step, 0)


def _gla(raw, bcum, act, gla_norm, *, batch, seq, heads_per_step=2, unroll=2):
    t = raw.shape[0]
    dk, dv = GLA_HEAD_K * heads_per_step, GLA_HEAD_V * heads_per_step
    assert RAW_GQ % dk == 0 and RAW_GK % dk == 0 and RAW_GV % dv == 0
    return pl.pallas_call(
        functools.partial(_gla_kernel, n_heads=heads_per_step, unroll=unroll),
        grid=(batch, GLA_HEADS // heads_per_step),
        in_specs=[pl.BlockSpec((seq, dk), lambda b, h: (b, RAW_GQ // dk + h)),
                  pl.BlockSpec((seq, dk), lambda b, h: (b, RAW_GK // dk + h)),
                  pl.BlockSpec((seq, dv), lambda b, h: (b, RAW_GV // dv + h)),
                  pl.BlockSpec((seq, dk), lambda b, h: (b, h)),
                  pl.BlockSpec((seq, dv), lambda b, h: (b, h)),
                  pl.BlockSpec((1, GLA_HEAD_V), lambda b, h: (0, 0))],
        out_specs=pl.BlockSpec((seq, dv), lambda b, h: (b, h)),
        out_shape=jax.ShapeDtypeStruct((t, GLA_HEADS * GLA_HEAD_V), BF16),
        scratch_shapes=[pltpu.VMEM((heads_per_step, GLA_HEAD_V, GLA_HEAD_K), F32)],
        compiler_params=_params(2),
        name="gla",
    )(raw, raw, raw, bcum, act, gla_norm)


def _merge_kernel(x_ref, of_ref, os_ref, og_ref, g0_ref, g1_ref, g2_ref,
                  wf_ref, ws_ref, wg_ref, wo_ref, o_ref):
    merged = (g0_ref[...].astype(F32) * _dot(of_ref[...], wf_ref[...])
              + g1_ref[...].astype(F32) * _dot(os_ref[...], ws_ref[...])
              + g2_ref[...].astype(F32) * _dot(og_ref[...], wg_ref[...]))
    o_ref[...] = x_ref[...] + _dot(merged.astype(BF16), wo_ref[...])


def _merge(x, o_fox, o_swa, o_gla, act, wf, ws, wg, wo, *, layer, tm=1024):
    t, d = x.shape
    row = lambda w, j=0: pl.BlockSpec((tm, w), lambda i: (i, j))
    return pl.pallas_call(
        _merge_kernel,
        grid=(t // tm,),
        in_specs=[row(d), row(o_fox.shape[1]), row(o_swa.shape[1]), row(o_gla.shape[1]),
                  row(d, 1), row(d, 2), row(d, 3),
                  _layer(wf.shape[1:], layer), _layer(ws.shape[1:], layer),
                  _layer(wg.shape[1:], layer), _layer(wo.shape[1:], layer)],
        out_specs=row(d),
        out_shape=jax.ShapeDtypeStruct((t, d), F32),
        compiler_params=_params(1),
        name="merge",
    )(x, o_fox, o_swa, o_gla, act, act, act, wf, ws, wg, wo)


def _w_in_pieces():
    sizes = (512, 512, 512, FOX_HEADS, 512, 128, 128, 512, 512, 1024, GLA_RANK, 1024, 3072)
    offs = [0]
    for s in sizes:
        offs.append(offs[-1] + s)
    fq, fk, fv, ff, sq, sk, sv, gq, gk, gv, glr, gr, gates = offs[:-1]
    h = HEAD_DIM
    raw = ([(fq, 512), (fk, 512), (fv, 512)] + [(sq + h * i, h) for i in SWA_HEAD_ORDER]
           + [(gq, 512), (gk, 512), (gv, 1024), (sk, 2 * h), (sv, 2 * h)])
    moves, dst = [], 0
    for src, width in raw:
        moves.append(("raw", dst, src, width))
        dst += width
    assert dst == RAW_WIDTH
    moves.append(("act", 0, gr, ACT_WIDTH))
    moves += [("small", 0, ff, FOX_HEADS), ("small", FOX_HEADS, glr, GLA_RANK)]
    return moves


def _w_in_kernel(w_ref, raw_ref, act_ref, small_ref):
    outs = {"raw": raw_ref, "act": act_ref, "small": small_ref}
    small_ref[...] = jnp.zeros(small_ref.shape, BF16)
    for name, dst, src, width in _w_in_pieces():
        outs[name][dst:dst + width, :] = w_ref[src:src + width, :].astype(BF16)


def _prep_w_in(w_in, *, cols=256):
    depth, d, d_in = w_in.shape
    w_t = jnp.swapaxes(w_in, 1, 2)
    out = lambda n: pl.BlockSpec((None, n, cols), lambda l, i: (l, 0, i))
    return pl.pallas_call(
        _w_in_kernel,
        grid=(depth, d // cols),
        in_specs=[pl.BlockSpec((None, d_in, cols), lambda l, i: (l, 0, i))],
        out_specs=[out(RAW_WIDTH), out(ACT_WIDTH), out(SMALL_WIDTH)],
        out_shape=[jax.ShapeDtypeStruct((depth, RAW_WIDTH, d), BF16),
                   jax.ShapeDtypeStruct((depth, ACT_WIDTH, d), BF16),
                   jax.ShapeDtypeStruct((depth, SMALL_WIDTH, d), BF16)],
        compiler_params=_params(2),
        name="w_in_layout",
    )(w_t)


def kernel(x, ffn1_norm, ffn1_w_gu, ffn1_w_down, mix_norm, w_in, fox_forget_bias, swa_sinks,
           gla_gate_w2, gla_gate_bias, gla_norm, w_branch_fox, w_branch_swa, w_branch_gla, w_out,
           ffn2_norm, ffn2_w_gu, ffn2_w_down, final_norm):
    batch, seq, d = x.shape
    depth = w_in.shape[0]
    xt = x.reshape(batch * seq, d)

    w_raw, w_act, w_small = _prep_w_in(w_in)
    gk = gla_gate_w2.shape[-1]
    w2t = jnp.zeros((depth, gk, SMALL_WIDTH), BF16)
    w2t = w2t.at[:, :, FOX_HEADS:FOX_HEADS + GLA_RANK].set(
        jnp.swapaxes(gla_gate_w2, 1, 2).astype(BF16))
    fbias_t = jnp.broadcast_to(fox_forget_bias[:, :, None], (depth, FOX_HEADS, LANES))
    gbias_t = jnp.broadcast_to(gla_gate_bias[:, :, None], (depth, gk, LANES))
    f1gu, f1d = ffn1_w_gu.astype(BF16), ffn1_w_down.astype(BF16)
    f2gu, f2d = ffn2_w_gu.astype(BF16), ffn2_w_down.astype(BF16)
    wbf = w_branch_fox.astype(BF16)
    swa_rows = w_branch_swa.reshape(depth, SWA_Q_HEADS, HEAD_DIM, d)[:, jnp.array(SWA_HEAD_ORDER)]
    wbs = swa_rows.reshape(depth, SWA_Q_HEADS * HEAD_DIM, d).astype(BF16)
    wbg, wo = w_branch_gla.astype(BF16), w_out.astype(BF16)
    fw = final_norm.reshape(1, d)

    for l in range(depth):
        xt = _ffn(xt, ffn1_norm[l].reshape(1, d), f1gu, f1d, fw, layer=l, final_norm=False)
        raw, act, small_t = _inproj(xt, mix_norm[l].reshape(1, d), w_raw, w_act, w_small, layer=l)
        c_split, bcum = _decay(small_t, w2t, fbias_t, gbias_t, batch=batch, seq=seq, layer=l)
        o_fox = _fox(raw, c_split, batch=batch, seq=seq)
        o_swa = _swa(raw, swa_sinks[l], batch=batch, seq=seq)
        o_gla = _gla(raw, bcum, act, gla_norm[l].reshape(1, -1), batch=batch, seq=seq)
        xt = _merge(xt, o_fox, o_swa, o_gla, act, wbf, wbs, wbg, wo, layer=l)
        xt = _ffn(xt, ffn2_norm[l].reshape(1, d), f2gu, f2d, fw, layer=l,
                  final_norm=(l == depth - 1))
    return xt.reshape(batch, seq, d)
```

```python
import functools

import jax
import jax.numpy as jnp
from jax import lax
from jax.experimental import pallas as pl
from jax.experimental.pallas import tpu as pltpu

F32 = jnp.float32
BF16 = jnp.bfloat16

EPS = 1e-6
NEG_INF = -1e30
HEAD_DIM = 64
FOX_HEADS = 8
SWA_Q_HEADS = 8
SWA_KV_HEADS = 2
WINDOW = 128
GLA_HEADS = 4
GLA_HEAD_K = 128
GLA_HEAD_V = 256
GLA_RANK = 16
GLA_TAU = 16.0
GLA_CHUNK = 64
GLA_SUPER = 256
FOX_Q_TILE = 1024
LOG2E = 1.4426950408889634
FOX_ONE_LANE = 3 * FOX_HEADS
FOX_Q_SCALE = HEAD_DIM ** -0.5 * LOG2E
SWA_UNROLL = 3
LANES = 128
VMEM_LIMIT_BYTES = 56 * 1024 * 1024

RAW_FQ, RAW_FK, RAW_FV = 0, 512, 1024
RAW_SQ = 1536
RAW_GQ, RAW_GK, RAW_GV = 2048, 2560, 3072
RAW_SK, RAW_SV = 4096, 4224
RAW_WIDTH = 4352
SWA_HEAD_ORDER = (0, 4, 1, 5, 2, 6, 3, 7)
ACT_WIDTH = 4096
SMALL_WIDTH = 128


def _params(n_grid_axes):
    return pltpu.CompilerParams(
        dimension_semantics=("arbitrary",) * n_grid_axes,
        vmem_limit_bytes=VMEM_LIMIT_BYTES)


def _resident(shape):
    nd = len(shape)
    return pl.BlockSpec(shape, lambda *_: (0,) * nd, pipeline_mode=pl.Buffered(1))


def _layer(shape, layer):
    nd = len(shape)
    return pl.BlockSpec((None,) + tuple(shape), lambda *_: (layer,) + (0,) * nd,
                        pipeline_mode=pl.Buffered(1))


def _rmsnorm(x, w):
    return x * lax.rsqrt(jnp.mean(x * x, axis=-1, keepdims=True) + EPS) * w


def _sigmoid(x):
    return 0.5 * jnp.tanh(0.5 * x) + 0.5


def _silu(x):
    h = 0.5 * x
    return h + h * jnp.tanh(h)


def _log2_sigmoid(x):
    return jnp.minimum(x, 0.0) * LOG2E - jnp.log2(1.0 + jnp.exp2(jnp.abs(x) * -LOG2E))


def _split3(x):
    p1 = x.astype(BF16)
    r1 = x - p1.astype(F32)
    p2 = r1.astype(BF16)
    p3 = (r1 - p2.astype(F32)).astype(BF16)
    return p1, p2, p3


def _lane_repeat(x, n):
    return jnp.concatenate([x] * n, axis=1)


def _dot(a, b):
    return jnp.dot(a, b, preferred_element_type=F32)


def _dot_nt(a, b):
    return lax.dot_general(a, b, (((1,), (1,)), ((), ())), preferred_element_type=F32)


def _dot_tn(a, b):
    return lax.dot_general(a, b, (((0,), (0,)), ((), ())), preferred_element_type=F32)


def _exact_tri_cumsum(tri, x):
    p1, p2, p3 = _split3(x)
    return _dot(tri, p1) + _dot(tri, p2) + _dot(tri, p3)


def _ffn_kernel(x_ref, nw_ref, wgu_ref, wd_ref, fw_ref, o_ref, h_ref, a_ref, *,
                f_chunk, final_norm):
    h_ref[...] = _rmsnorm(x_ref[...], nw_ref[...]).astype(BF16)
    d_ff = wd_ref.shape[0]
    for c in range(d_ff // f_chunk):
        sl = slice(c * f_chunk, (c + 1) * f_chunk)
        g = _dot(h_ref[...], wgu_ref[:, sl])
        u = _dot(h_ref[...], wgu_ref[:, d_ff + c * f_chunk:d_ff + (c + 1) * f_chunk])
        a_ref[:, sl] = (_silu(g) * u).astype(BF16)
    y = x_ref[...] + 0.5 * _dot(a_ref[...], wd_ref[...])
    if final_norm:
        y = _rmsnorm(y, fw_ref[...])
    o_ref[...] = y


def _ffn(x, nw, wgu, wd, fw, *, layer, final_norm, tm=1024, f_chunk=256):
    t, d = x.shape
    d_ff = wd.shape[1]
    row = pl.BlockSpec((tm, d), lambda i: (i, 0))
    return pl.pallas_call(
        functools.partial(_ffn_kernel, f_chunk=f_chunk, final_norm=final_norm),
        grid=(t // tm,),
        in_specs=[row, _resident((1, d)), _layer((d, 2 * d_ff), layer),
                  _layer((d_ff, d), layer), _resident((1, d))],
        out_specs=row,
        out_shape=jax.ShapeDtypeStruct((t, d), F32),
        scratch_shapes=[pltpu.VMEM((tm, d), BF16), pltpu.VMEM((tm, d_ff), BF16)],
        compiler_params=_params(1),
        name="ffn",
    )(x, nw, wgu, wd, fw)


def _inproj_kernel(x_ref, nw_ref, wraw_ref, wact_ref, wsm_ref,
                   raw_ref, act_ref, small_t_ref, h_ref, *, n_chunk, gr_width):
    h_ref[...] = _rmsnorm(x_ref[...], nw_ref[...]).astype(BF16)
    for start in range(0, raw_ref.shape[1], n_chunk):
        sl = slice(start, min(start + n_chunk, raw_ref.shape[1]))
        z = _dot_nt(h_ref[...], wraw_ref[sl, :])
        if sl.stop <= RAW_FK:
            z = z * FOX_Q_SCALE
        elif RAW_GQ <= sl.start and sl.stop <= RAW_GK:
            z = z * (GLA_HEAD_K ** -0.5)
        raw_ref[:, sl] = z.astype(BF16)
    for c in range(act_ref.shape[1] // n_chunk):
        sl = slice(c * n_chunk, (c + 1) * n_chunk)
        z = _dot_nt(h_ref[...], wact_ref[sl, :])
        if c * n_chunk < gr_width:
            act_ref[:, sl] = _silu(z).astype(BF16)
        else:
            act_ref[:, sl] = _sigmoid(z).astype(BF16)
    small_t_ref[...] = _dot_nt(wsm_ref[...], h_ref[...])


def _inproj(x, nw, w_raw, w_act, w_small, *, layer, tm=512, n_chunk=512):
    t, d = x.shape
    row = lambda w: pl.BlockSpec((tm, w), lambda i: (i, 0))
    return pl.pallas_call(
        functools.partial(_inproj_kernel, n_chunk=n_chunk, gr_width=GLA_HEADS * GLA_HEAD_V),
        grid=(t // tm,),
        in_specs=[row(d), _resident((1, d)), _layer((RAW_WIDTH, d), layer),
                  _layer((ACT_WIDTH, d), layer), _layer((SMALL_WIDTH, d), layer)],
        out_specs=[row(RAW_WIDTH), row(ACT_WIDTH),
                   pl.BlockSpec((SMALL_WIDTH, tm), lambda i: (0, i))],
        out_shape=[jax.ShapeDtypeStruct((t, RAW_WIDTH), BF16),
                   jax.ShapeDtypeStruct((t, ACT_WIDTH), BF16),
                   jax.ShapeDtypeStruct((SMALL_WIDTH, t), F32)],
        scratch_shapes=[pltpu.VMEM((tm, d), BF16)],
        compiler_params=_params(1),
        name="inproj",
    )(x, nw, w_raw, w_act, w_small)


def _decay_kernel(small_t_ref, w2t_ref, fbt_ref, gbt_ref, p_ref, bcum_ref):
    s_len = small_t_ref.shape[1]
    blk = LANES
    n_heads = FOX_HEADS
    n_blk = s_len // blk
    r = lax.broadcasted_iota(jnp.int32, (blk, 2 * blk), 0)
    cc = lax.broadcasted_iota(jnp.int32, (blk, 2 * blk), 1)
    upper_total = jnp.where((r <= cc) | (cc >= blk), 1.0, 0.0).astype(BF16)
    rr = lax.broadcasted_iota(jnp.int32, (4 * n_heads, blk), 0)
    ll = lax.broadcasted_iota(jnp.int32, (4 * n_heads, blk), 1)
    place = jnp.where(rr == ll, 1.0, 0.0).astype(BF16)
    one_row = lax.broadcasted_iota(jnp.int32, (n_heads, blk), 0) == FOX_ONE_LANE - 3 * n_heads

    sums = []
    for n in range(n_blk):
        cols = slice(n * blk, (n + 1) * blk)
        log2_f = _log2_sigmoid(small_t_ref[0:n_heads, cols] + fbt_ref[...])
        sums.append(sum(_dot(p, upper_total) for p in _split3(log2_f)))
    carry = jnp.zeros((n_heads, blk), F32)
    for n in range(n_blk):
        within = sums[n][:, :blk] + carry
        carry = carry + sums[n][:, blk:]
        c1, c2, c3 = _split3(within)
        ones = jnp.where(one_row, 1.0, 0.0).astype(BF16)
        stacked = jnp.concatenate([c1, c2, c3, ones], axis=0)
        p_ref[n * blk:(n + 1) * blk, :] = _dot_tn(stacked, place).astype(BF16)

    r = lax.broadcasted_iota(jnp.int32, (GLA_SUPER, GLA_SUPER), 0)
    cc = lax.broadcasted_iota(jnp.int32, (GLA_SUPER, GLA_SUPER), 1)
    upper = jnp.where((r <= cc) & ((r // GLA_CHUNK) == (cc // GLA_CHUNK)), 1.0, 0.0).astype(BF16)
    gate_bias = _lane_repeat(gbt_ref[...], GLA_SUPER // LANES)
    n_sup = s_len // GLA_SUPER

    def gate(sb):
        cols = slice(sb * GLA_SUPER, (sb + 1) * GLA_SUPER)
        return _dot(w2t_ref[...], small_t_ref[:, cols].astype(BF16)) + gate_bias

    def split(gate_logit):
        log2_a = _log2_sigmoid(gate_logit) * (1.0 / GLA_TAU)
        hi = log2_a.astype(BF16)
        return hi, (log2_a - hi.astype(F32)).astype(BF16)

    def cumsum(sb, parts):
        rows = slice(sb * GLA_SUPER, (sb + 1) * GLA_SUPER)
        bcum_ref[rows, :] = (_dot(parts[0], upper) + _dot(parts[1], upper)).T

    logits, parts = {}, {}
    for step in range(n_sup + 2):
        if step < n_sup:
            logits[step] = gate(step)
        if 1 <= step <= n_sup:
            parts[step - 1] = split(logits.pop(step - 1))
        if step >= 2:
            cumsum(step - 2, parts.pop(step - 2))


def _decay(small_t, w2t, fbias_t, gbias_t, *, batch, seq, layer):
    t = small_t.shape[1]
    gk = w2t.shape[1]
    return pl.pallas_call(
        _decay_kernel,
        grid=(batch,),
        in_specs=[pl.BlockSpec((SMALL_WIDTH, seq), lambda b: (0, b)),
                  _layer((gk, SMALL_WIDTH), layer), _layer((FOX_HEADS, LANES), layer),
                  _layer((gk, LANES), layer)],
        out_specs=[pl.BlockSpec((seq, LANES), lambda b: (b, 0)),
                   pl.BlockSpec((seq, gk), lambda b: (b, 0))],
        out_shape=[jax.ShapeDtypeStruct((t, LANES), BF16), jax.ShapeDtypeStruct((t, gk), F32)],
        compiler_params=_params(1),
        name="decay",
    )(small_t, w2t, fbias_t, gbias_t)


def _fox_kernel(q_ref, k_ref, v_ref, p_ref, o_ref,
                qa_ref, qb_ref, ka_ref, kb_ref, va_ref, vb_ref, m_ref, acc_ref, *, tq):
    s_len = q_ref.shape[0]
    pair = pl.program_id(1)
    n_heads = FOX_HEADS
    one_lane = FOX_ONE_LANE
    lane = lax.broadcasted_iota(jnp.int32, (1, LANES), 1)

    row = lax.broadcasted_iota(jnp.int32, (LANES, 2 * LANES), 0)
    col = lax.broadcasted_iota(jnp.int32, (LANES, 2 * LANES), 1)
    k_side = col >= LANES
    dst = col % LANES
    aug = dst % HEAD_DIM
    head = 2 * pair + jnp.where(dst < HEAD_DIM, 1, 0)
    part = aug - jnp.where(k_side, 3, 0)
    from_c = (part >= 0) & (part < 3) & (row == head + n_heads * part)
    ones_pos = aug - jnp.where(k_side, 0, 3)
    from_one = (row == one_lane) & (ones_pos >= 0) & (ones_pos < 3)
    sel = jnp.where(from_c, jnp.where(k_side, -1.0, 1.0), jnp.where(from_one, 1.0, 0.0))
    spare = _dot(p_ref[...], sel.astype(BF16))
    q_spare = spare[:, :LANES].astype(BF16)
    k_spare = spare[:, LANES:].astype(BF16)

    low = lax.broadcasted_iota(jnp.int32, (s_len, LANES), 1) < HEAD_DIM
    q, k, v = q_ref[...], k_ref[...], v_ref[...]
    qa_ref[...] = jnp.where(low, q, q_spare)
    qb_ref[...] = jnp.where(low, q_spare, q)
    ka_ref[...] = jnp.where(low, k, k_spare)
    kb_ref[...] = jnp.where(low, k_spare, k)
    va_ref[...] = jnp.where(low, v, jnp.where(lane == HEAD_DIM, 1.0, 0.0).astype(BF16))
    vb_ref[...] = jnp.where(low, jnp.where(lane == 0, 1.0, 0.0).astype(BF16), v)

    heads = ((qa_ref, ka_ref, va_ref), (qb_ref, kb_ref, vb_ref))

    def scores(h, q0, nq, k_rows, bias):
        q_ref_h, k_ref_h, _ = heads[h]
        s = _dot_nt(q_ref_h[q0:q0 + nq, :], k_ref_h[k_rows, :])
        return s if bias is None else s + bias

    def start(h, q0, nq, k_rows, bias):
        s = scores(h, q0, nq, k_rows, bias)
        m = jnp.max(s, axis=-1, keepdims=True) + jnp.zeros((nq, LANES), F32)
        p = jnp.exp2(s - _lane_repeat(m, s.shape[1] // LANES))
        acc_ref[h, q0:q0 + nq, :] = _dot(p.astype(BF16), heads[h][2][k_rows, :])
        m_ref[h, q0:q0 + nq, :] = m

    def update(h, q0, nq, k_rows):
        s = scores(h, q0, nq, k_rows, None)
        m_prev = m_ref[h, q0:q0 + nq, :]
        m_new = jnp.maximum(m_prev, jnp.max(s, axis=-1, keepdims=True))
        alpha = jnp.exp2(m_prev - m_new)
        p = jnp.exp2(s - _lane_repeat(m_new, s.shape[1] // LANES))
        acc_ref[h, q0:q0 + nq, :] = (alpha * acc_ref[h, q0:q0 + nq, :]
                                     + _dot(p.astype(BF16), heads[h][2][k_rows, :]))
        m_ref[h, q0:q0 + nq, :] = m_new

    half = tq // 2
    ri = lax.broadcasted_iota(jnp.int32, (half, tq), 0)
    ci = lax.broadcasted_iota(jnp.int32, (half, tq), 1)
    upper_bias = jnp.where(lax.broadcasted_iota(jnp.int32, (half, half), 1)
                           <= lax.broadcasted_iota(jnp.int32, (half, half), 0), 0.0, NEG_INF)
    lower_bias = jnp.where(ci <= ri + half, 0.0, NEG_INF)
    n_tiles = s_len // tq
    for qi in range(n_tiles):
        q0 = qi * tq
        for h in range(2):
            start(h, q0, half, slice(q0, q0 + half), upper_bias)
            start(h, q0 + half, half, slice(q0, q0 + tq), lower_bias)

    for qi in range(1, n_tiles):
        def kv_step(kj, _, q0=qi * tq):
            k_rows = pl.ds(pl.multiple_of(kj * tq, tq), tq)
            for h in range(2):
                update(h, q0, tq, k_rows)
            return 0

        lax.fori_loop(0, qi, kv_step, 0)

    acc_a, acc_b = acc_ref[0], acc_ref[1]
    l_a = jnp.sum(jnp.where(lane == HEAD_DIM, acc_a, 0.0), axis=-1, keepdims=True)
    l_b = jnp.sum(jnp.where(lane == 0, acc_b, 0.0), axis=-1, keepdims=True)
    o_ref[...] = jnp.where(lane < HEAD_DIM, acc_a / l_a, acc_b / l_b).astype(BF16)


def _fox(raw, c_split, *, batch, seq):
    t = raw.shape[0]
    tq = min(FOX_Q_TILE, seq)
    n_pairs = FOX_HEADS // 2
    blk = lambda off: pl.BlockSpec((seq, LANES), lambda b, j: (b, off // LANES + j))
    return pl.pallas_call(
        functools.partial(_fox_kernel, tq=tq),
        grid=(batch, n_pairs),
        in_specs=[blk(RAW_FQ), blk(RAW_FK), blk(RAW_FV),
                  pl.BlockSpec((seq, LANES), lambda b, j: (b, 0))],
        out_specs=pl.BlockSpec((seq, LANES), lambda b, j: (b, j)),
        out_shape=jax.ShapeDtypeStruct((t, FOX_HEADS * HEAD_DIM), BF16),
        scratch_shapes=[pltpu.VMEM((seq, LANES), BF16)] * 6
                       + [pltpu.VMEM((2, seq, LANES), F32)] * 2,
        compiler_params=_params(2),
        name="fox",
    )(raw, raw, raw, c_split)


def _swa_kernel(sink_ref, q_ref, k_ref, v_ref, o_ref):
    s_len = q_ref.shape[0]
    w = WINDOW
    group = SWA_Q_HEADS // SWA_KV_HEADS
    scale = HEAD_DIM ** -0.5 * LOG2E
    low = lax.broadcasted_iota(jnp.int32, (w, LANES), 1) < HEAD_DIM
    i = lax.broadcasted_iota(jnp.int32, (w, 2 * w), 0)
    j = lax.broadcasted_iota(jnp.int32, (w, 2 * w), 1)
    band = jnp.where((j > i) & (j <= i + w), 0.0, NEG_INF)
    band = jnp.concatenate([band] * group, axis=0)
    it = lax.broadcasted_iota(jnp.int32, (w, w), 0)
    jt = lax.broadcasted_iota(jnp.int32, (w, w), 1)
    tri = jnp.concatenate([jnp.where(jt <= it, 0.0, NEG_INF)] * group, axis=0)
    hrow = lax.broadcasted_iota(jnp.int32, (group * w, LANES), 0) // w

    def block(n, sinks, first):
        q_rows = pl.ds(pl.multiple_of(n * w, w), w)
        k_rows = q_rows if first else pl.ds(pl.multiple_of((n - 1) * w, w), 2 * w)
        kk = k_ref[k_rows, :]
        vv = v_ref[k_rows, :]
        q_blocks = [q_ref[q_rows, g * LANES:(g + 1) * LANES] for g in range(group)]
        outs = []
        for kvh in range(SWA_KV_HEADS):
            keep = low if kvh == 0 else jnp.logical_not(low)
            q = jnp.concatenate([jnp.where(keep, qb, jnp.zeros_like(qb)) for qb in q_blocks],
                                axis=0)
            s = _dot_nt(q, kk) * scale + (tri if first else band)
            m = jnp.maximum(jnp.max(s, axis=-1, keepdims=True), sinks[kvh])
            p = jnp.exp2(s - (m if first else _lane_repeat(m, 2 * w // LANES)))
            denom = jnp.sum(p, axis=-1, keepdims=True) + jnp.exp2(sinks[kvh] - m)
            outs.append(_dot(p.astype(BF16), vv) / denom)
        for g in range(group):
            rows = slice(g * w, (g + 1) * w)
            o_ref[q_rows, g * LANES:(g + 1) * LANES] = jnp.where(
                low, outs[0][rows, :], outs[1][rows, :]).astype(BF16)

    sinks = []
    for kvh in range(SWA_KV_HEADS):
        sink = jnp.zeros((group * w, LANES), F32)
        for g in range(group):
            sink = jnp.where(hrow == g, sink_ref[kvh * group + g] * LOG2E, sink)
        sinks.append(sink)
    block(0, sinks, True)

    def body(n, _):
        block(n, sinks, False)
        return 0

    lax.fori_loop(1, s_len // w, body, 0, unroll=SWA_UNROLL)


def _swa(raw, sinks, *, batch, seq):
    t = raw.shape[0]
    qw = SWA_Q_HEADS * HEAD_DIM
    kw = SWA_KV_HEADS * HEAD_DIM
    return pl.pallas_call(
        _swa_kernel,
        grid=(batch,),
        in_specs=[pl.BlockSpec(memory_space=pltpu.SMEM),
                  pl.BlockSpec((seq, qw), lambda b: (b, RAW_SQ // qw)),
                  pl.BlockSpec((seq, kw), lambda b: (b, RAW_SK // kw)),
                  pl.BlockSpec((seq, kw), lambda b: (b, RAW_SV // kw))],
        out_specs=pl.BlockSpec((seq, qw), lambda b: (b, 0)),
        out_shape=jax.ShapeDtypeStruct((t, qw), BF16),
        compiler_params=_params(1),
        name="swa",
    )(sinks, raw, raw, raw)


def _gla_kernel(q_ref, k_ref, v_ref, b_ref, gr_ref, nw_ref, o_ref, st_ref, *, n_heads, unroll):
    s_len = q_ref.shape[0]
    ck, sup = GLA_CHUNK, GLA_SUPER
    dk, dv = GLA_HEAD_K, GLA_HEAD_V
    per = sup // ck
    r = lax.broadcasted_iota(jnp.int32, (sup, sup), 0)
    c = lax.broadcasted_iota(jnp.int32, (sup, sup), 1)
    intra = (r >= c) & ((r // ck) == (c // ck))
    st_ref[...] = jnp.zeros(st_ref.shape, F32)

    def decayed(rows, h):
        kcols = slice(h * dk, (h + 1) * dk)
        b = b_ref[rows, kcols]
        b3 = b.reshape(per, ck, dk)
        b_last = b3[:, ck - 1:ck, :]
        qf = q_ref[rows, kcols].astype(F32)
        kf = k_ref[rows, kcols].astype(F32)
        q_t = (qf * jnp.exp2(b)).astype(BF16)
        k_t = (kf * jnp.exp2(-b)).astype(BF16)
        k_s = (kf.reshape(per, ck, dk) * jnp.exp2(b_last - b3)).astype(BF16)
        return q_t, k_t, k_s, jnp.exp2(b_last), v_ref[rows, h * dv:(h + 1) * dv]

    def products(q_t, k_t, k_s, decay, v):
        qk = _dot_nt(q_t, k_t)
        kv = [_dot_tn(v[c * ck:(c + 1) * ck, :], k_s[c]) for c in range(per)]
        return qk, kv

    def intra_chunk(qk, v):
        return _dot(jnp.where(intra, qk, 0.0).astype(BF16), v)

    def carry_state(h, q_t, decay, kv, o_intra):
        st = st_ref[h]
        outs = []
        for c in range(per):
            chunk = slice(c * ck, (c + 1) * ck)
            outs.append(o_intra[chunk, :] + _dot_nt(q_t[chunk, :], st.astype(BF16)))
            st = st * decay[c] + kv[c]
        st_ref[h] = st
        return jnp.concatenate(outs, axis=0)

    def finish(rows, h, o):
        vcols = slice(h * dv, (h + 1) * dv)
        o = o * lax.rsqrt(jnp.mean(o * o, axis=-1, keepdims=True) + EPS) * nw_ref[...]
        o_ref[rows, vcols] = (o * gr_ref[rows, vcols].astype(F32)).astype(BF16)

    def step(i, _):
        work = []
        for u in range(unroll):
            start = pl.multiple_of((i * unroll + u) * sup, sup)
            work += [(pl.ds(start, sup), h) for h in range(n_heads)]
        pre = [decayed(rows, h) for rows, h in work]
        prod = [products(*p) for p in pre]
        o_intra = [intra_chunk(qk, p[4]) for (qk, _), p in zip(prod, pre)]
        outs = [carry_state(h, p[0], p[3], kv, oi)
                for (_, h), p, (_, kv), oi in zip(work, pre, prod, o_intra)]
        for (rows, h), o in zip(work, outs):
            finish(rows, h, o)
        return 0

    lax.fori_loop(0, s_len // (sup * unroll), step, 0)


def _gla(raw, bcum, act, gla_norm, *, batch, seq, heads_per_step=2, unroll=4):
    t = raw.shape[0]
    dk, dv = GLA_HEAD_K * heads_per_step, GLA_HEAD_V * heads_per_step
    assert RAW_GQ % dk == 0 and RAW_GK % dk == 0 and RAW_GV % dv == 0
    return pl.pallas_call(
        functools.partial(_gla_kernel, n_heads=heads_per_step, unroll=unroll),
        grid=(batch, GLA_HEADS // heads_per_step),
        in_specs=[pl.BlockSpec((seq, dk), lambda b, h: (b, RAW_GQ // dk + h)),
                  pl.BlockSpec((seq, dk), lambda b, h: (b, RAW_GK // dk + h)),
                  pl.BlockSpec((seq, dv), lambda b, h: (b, RAW_GV // dv + h)),
                  pl.BlockSpec((seq, dk), lambda b, h: (b, h)),
                  pl.BlockSpec((seq, dv), lambda b, h: (b, h)),
                  pl.BlockSpec((1, GLA_HEAD_V), lambda b, h: (0, 0))],
        out_specs=pl.BlockSpec((seq, dv), lambda b, h: (b, h)),
        out_shape=jax.ShapeDtypeStruct((t, GLA_HEADS * GLA_HEAD_V), BF16),
        scratch_shapes=[pltpu.VMEM((heads_per_step, GLA_HEAD_V, GLA_HEAD_K), F32)],
        compiler_params=_params(2),
        name="gla",
    )(raw, raw, raw, bcum, act, gla_norm)


def _merge_kernel(x_ref, of_ref, os_ref, og_ref, g0_ref, g1_ref, g2_ref,
                  wf_ref, ws_ref, wg_ref, wo_ref, o_ref):
    merged = (g0_ref[...].astype(F32) * _dot(of_ref[...], wf_ref[...])
              + g1_ref[...].astype(F32) * _dot(os_ref[...], ws_ref[...])
              + g2_ref[...].astype(F32) * _dot(og_ref[...], wg_ref[...]))
    o_ref[...] = x_ref[...] + _dot(merged.astype(BF16), wo_ref[...])


def _merge(x, o_fox, o_swa, o_gla, act, wf, ws, wg, wo, *, layer, tm=1024):
    t, d = x.shape
    row = lambda w, j=0: pl.BlockSpec((tm, w), lambda i: (i, j))
    return pl.pallas_call(
        _merge_kernel,
        grid=(t // tm,),
        in_specs=[row(d), row(o_fox.shape[1]), row(o_swa.shape[1]), row(o_gla.shape[1]),
                  row(d, 1), row(d, 2), row(d, 3),
                  _layer(wf.shape[1:], layer), _layer(ws.shape[1:], layer),
                  _layer(wg.shape[1:], layer), _layer(wo.shape[1:], layer)],
        out_specs=row(d),
        out_shape=jax.ShapeDtypeStruct((t, d), F32),
        compiler_params=_params(1),
        name="merge",
    )(x, o_fox, o_swa, o_gla, act, act, act, wf, ws, wg, wo)


def _w_in_pieces():
    sizes = (512, 512, 512, FOX_HEADS, 512, 128, 128, 512, 512, 1024, GLA_RANK, 1024, 3072)
    offs = [0]
    for s in sizes:
        offs.append(offs[-1] + s)
    fq, fk, fv, ff, sq, sk, sv, gq, gk, gv, glr, gr, gates = offs[:-1]
    h = HEAD_DIM
    raw = ([(fq, 512), (fk, 512), (fv, 512)] + [(sq + h * i, h) for i in SWA_HEAD_ORDER]
           + [(gq, 512), (gk, 512), (gv, 1024), (sk, 2 * h), (sv, 2 * h)])
    moves, dst = [], 0
    for src, width in raw:
        moves.append(("raw", dst, src, width))
        dst += width
    assert dst == RAW_WIDTH
    moves.append(("act", 0, gr, ACT_WIDTH))
    moves += [("small", 0, ff, FOX_HEADS), ("small", FOX_HEADS, glr, GLA_RANK)]
    return moves


def _w_in_kernel(w_ref, raw_ref, act_ref, small_ref):
    outs = {"raw": raw_ref, "act": act_ref, "small": small_ref}
    small_ref[...] = jnp.zeros(small_ref.shape, BF16)
    for name, dst, src, width in _w_in_pieces():
        outs[name][dst:dst + width, :] = w_ref[src:src + width, :].astype(BF16)


def _prep_w_in(w_in, *, cols=256):
    depth, d, d_in = w_in.shape
    w_t = jnp.swapaxes(w_in, 1, 2)
    out = lambda n: pl.BlockSpec((None, n, cols), lambda l, i: (l, 0, i))
    return pl.pallas_call(
        _w_in_kernel,
        grid=(depth, d // cols),
        in_specs=[pl.BlockSpec((None, d_in, cols), lambda l, i: (l, 0, i))],
        out_specs=[out(RAW_WIDTH), out(ACT_WIDTH), out(SMALL_WIDTH)],
        out_shape=[jax.ShapeDtypeStruct((depth, RAW_WIDTH, d), BF16),
                   jax.ShapeDtypeStruct((depth, ACT_WIDTH, d), BF16),
                   jax.ShapeDtypeStruct((depth, SMALL_WIDTH, d), BF16)],
        compiler_params=_params(2),
        name="w_in_layout",
    )(w_t)


def kernel(x, ffn1_norm, ffn1_w_gu, ffn1_w_down, mix_norm, w_in, fox_forget_bias, swa_sinks,
           gla_gate_w2, gla_gate_bias, gla_norm, w_branch_fox, w_branch_swa, w_branch_gla, w_out,
           ffn2_norm, ffn2_w_gu, ffn2_w_down, final_norm):
    batch, seq, d = x.shape
    depth = w_in.shape[0]
    xt = x.reshape(batch * seq, d)

    w_raw, w_act, w_small = _prep_w_in(w_in)
    gk = gla_gate_w2.shape[-1]
    w2t = jnp.zeros((depth, gk, SMALL_WIDTH), BF16)
    w2t = w2t.at[:, :, FOX_HEADS:FOX_HEADS + GLA_RANK].set(
        jnp.swapaxes(gla_gate_w2, 1, 2).astype(BF16))
    fbias_t = jnp.broadcast_to(fox_forget_bias[:, :, None], (depth, FOX_HEADS, LANES))
    gbias_t = jnp.broadcast_to(gla_gate_bias[:, :, None], (depth, gk, LANES))
    f1gu, f1d = ffn1_w_gu.astype(BF16), ffn1_w_down.astype(BF16)
    f2gu, f2d = ffn2_w_gu.astype(BF16), ffn2_w_down.astype(BF16)
    wbf = w_branch_fox.astype(BF16)
    swa_rows = w_branch_swa.reshape(depth, SWA_Q_HEADS, HEAD_DIM, d)[:, jnp.array(SWA_HEAD_ORDER)]
    wbs = swa_rows.reshape(depth, SWA_Q_HEADS * HEAD_DIM, d).astype(BF16)
    wbg, wo = w_branch_gla.astype(BF16), w_out.astype(BF16)
    fw = final_norm.reshape(1, d)

    for l in range(depth):
        xt = _ffn(xt, ffn1_norm[l].reshape(1, d), f1gu, f1d, fw, layer=l, final_norm=False)
        raw, act, small_t = _inproj(xt, mix_norm[l].reshape(1, d), w_raw, w_act, w_small, layer=l)
        c_split, bcum = _decay(small_t, w2t, fbias_t, gbias_t, batch=batch, seq=seq, layer=l)
        o_fox = _fox(raw, c_split, batch=batch, seq=seq)
        o_swa = _swa(raw, swa_sinks[l], batch=batch, seq=seq)
        o_gla = _gla(raw, bcum, act, gla_norm[l].reshape(1, -1), batch=batch, seq=seq)
        xt = _merge(xt, o_fox, o_swa, o_gla, act, wbf, wbs, wbg, wo, layer=l)
        xt = _ffn(xt, ffn2_norm[l].reshape(1, d), f2gu, f2d, fw, layer=l,
                  final_norm=(l == depth - 1))
    return xt.reshape(batch, seq, d)
```

```python
import functools

import jax
import jax.numpy as jnp
from jax import lax
from jax.experimental import pallas as pl
from jax.experimental.pallas import tpu as pltpu

F32 = jnp.float32
BF16 = jnp.bfloat16

EPS = 1e-6
NEG_INF = -1e30
HEAD_DIM = 64
FOX_HEADS = 8
SWA_Q_HEADS = 8
SWA_KV_HEADS = 2
WINDOW = 128
GLA_HEADS = 4
GLA_HEAD_K = 128
GLA_HEAD_V = 256
GLA_RANK = 16
GLA_TAU = 16.0
GLA_CHUNK = 64
GLA_SUPER = 256
FOX_Q_TILE = 1024
LOG2E = 1.4426950408889634
FOX_ONE_LANE = 3 * FOX_HEADS
FOX_Q_SCALE = HEAD_DIM ** -0.5 * LOG2E
SWA_UNROLL = 5
LANES = 128
VMEM_LIMIT_BYTES = 56 * 1024 * 1024

RAW_FQ, RAW_FK, RAW_FV = 0, 512, 1024
RAW_SQ = 1536
RAW_GQ, RAW_GK, RAW_GV = 2048, 2560, 3072
RAW_SK, RAW_SV = 4096, 4224
RAW_WIDTH = 4352
SWA_HEAD_ORDER = (0, 4, 1, 5, 2, 6, 3, 7)
ACT_WIDTH = 4096
SMALL_WIDTH = 128


def _params(n_grid_axes):
    return pltpu.CompilerParams(
        dimension_semantics=("arbitrary",) * n_grid_axes,
        vmem_limit_bytes=VMEM_LIMIT_BYTES)


def _resident(shape):
    nd = len(shape)
    return pl.BlockSpec(shape, lambda *_: (0,) * nd, pipeline_mode=pl.Buffered(1))


def _layer(shape, layer):
    nd = len(shape)
    return pl.BlockSpec((None,) + tuple(shape), lambda *_: (layer,) + (0,) * nd,
                        pipeline_mode=pl.Buffered(1))


def _rmsnorm(x, w):
    return x * lax.rsqrt(jnp.mean(x * x, axis=-1, keepdims=True) + EPS) * w


def _sigmoid(x):
    return 0.5 * jnp.tanh(0.5 * x) + 0.5


def _silu(x):
    h = 0.5 * x
    return h + h * jnp.tanh(h)


def _log2_sigmoid(x):
    return jnp.minimum(x, 0.0) * LOG2E - jnp.log2(1.0 + jnp.exp2(jnp.abs(x) * -LOG2E))


def _split3(x):
    p1 = x.astype(BF16)
    r1 = x - p1.astype(F32)
    p2 = r1.astype(BF16)
    p3 = (r1 - p2.astype(F32)).astype(BF16)
    return p1, p2, p3


def _lane_repeat(x, n):
    return jnp.concatenate([x] * n, axis=1)


def _dot(a, b):
    return jnp.dot(a, b, preferred_element_type=F32)


def _dot_nt(a, b):
    return lax.dot_general(a, b, (((1,), (1,)), ((), ())), preferred_element_type=F32)


def _dot_tn(a, b):
    return lax.dot_general(a, b, (((0,), (0,)), ((), ())), preferred_element_type=F32)


def _exact_tri_cumsum(tri, x):
    p1, p2, p3 = _split3(x)
    return _dot(tri, p1) + _dot(tri, p2) + _dot(tri, p3)


def _ffn_kernel(x_ref, nw_ref, wgu_ref, wd_ref, fw_ref, o_ref, h_ref, a_ref, *,
                f_chunk, final_norm):
    h_ref[...] = _rmsnorm(x_ref[...], nw_ref[...]).astype(BF16)
    d_ff = wd_ref.shape[0]
    for c in range(d_ff // f_chunk):
        sl = slice(c * f_chunk, (c + 1) * f_chunk)
        g = _dot(h_ref[...], wgu_ref[:, sl])
        u = _dot(h_ref[...], wgu_ref[:, d_ff + c * f_chunk:d_ff + (c + 1) * f_chunk])
        a_ref[:, sl] = (_silu(g) * u).astype(BF16)
    y = x_ref[...] + 0.5 * _dot(a_ref[...], wd_ref[...])
    if final_norm:
        y = _rmsnorm(y, fw_ref[...])
    o_ref[...] = y


def _ffn(x, nw, wgu, wd, fw, *, layer, final_norm, tm=1024, f_chunk=256):
    t, d = x.shape
    d_ff = wd.shape[1]
    row = pl.BlockSpec((tm, d), lambda i: (i, 0))
    return pl.pallas_call(
        functools.partial(_ffn_kernel, f_chunk=f_chunk, final_norm=final_norm),
        grid=(t // tm,),
        in_specs=[row, _resident((1, d)), _layer((d, 2 * d_ff), layer),
                  _layer((d_ff, d), layer), _resident((1, d))],
        out_specs=row,
        out_shape=jax.ShapeDtypeStruct((t, d), F32),
        scratch_shapes=[pltpu.VMEM((tm, d), BF16), pltpu.VMEM((tm, d_ff), BF16)],
        compiler_params=_params(1),
        name="ffn",
    )(x, nw, wgu, wd, fw)


def _inproj_kernel(x_ref, nw_ref, wraw_ref, wact_ref, wsm_ref,
                   raw_ref, act_ref, small_t_ref, h_ref, *, n_chunk, gr_width):
    h_ref[...] = _rmsnorm(x_ref[...], nw_ref[...]).astype(BF16)
    for start in range(0, raw_ref.shape[1], n_chunk):
        sl = slice(start, min(start + n_chunk, raw_ref.shape[1]))
        z = _dot_nt(h_ref[...], wraw_ref[sl, :])
        if sl.stop <= RAW_FK:
            z = z * FOX_Q_SCALE
        elif RAW_GQ <= sl.start and sl.stop <= RAW_GK:
            z = z * (GLA_HEAD_K ** -0.5)
        raw_ref[:, sl] = z.astype(BF16)
    for c in range(act_ref.shape[1] // n_chunk):
        sl = slice(c * n_chunk, (c + 1) * n_chunk)
        z = _dot_nt(h_ref[...], wact_ref[sl, :])
        if c * n_chunk < gr_width:
            act_ref[:, sl] = _silu(z).astype(BF16)
        else:
            act_ref[:, sl] = _sigmoid(z).astype(BF16)
    small_t_ref[...] = _dot_nt(wsm_ref[...], h_ref[...])


def _inproj(x, nw, w_raw, w_act, w_small, *, layer, tm=512, n_chunk=512):
    t, d = x.shape
    row = lambda w: pl.BlockSpec((tm, w), lambda i: (i, 0))
    return pl.pallas_call(
        functools.partial(_inproj_kernel, n_chunk=n_chunk, gr_width=GLA_HEADS * GLA_HEAD_V),
        grid=(t // tm,),
        in_specs=[row(d), _resident((1, d)), _layer((RAW_WIDTH, d), layer),
                  _layer((ACT_WIDTH, d), layer), _layer((SMALL_WIDTH, d), layer)],
        out_specs=[row(RAW_WIDTH), row(ACT_WIDTH),
                   pl.BlockSpec((SMALL_WIDTH, tm), lambda i: (0, i))],
        out_shape=[jax.ShapeDtypeStruct((t, RAW_WIDTH), BF16),
                   jax.ShapeDtypeStruct((t, ACT_WIDTH), BF16),
                   jax.ShapeDtypeStruct((SMALL_WIDTH, t), F32)],
        scratch_shapes=[pltpu.VMEM((tm, d), BF16)],
        compiler_params=_params(1),
        name="inproj",
    )(x, nw, w_raw, w_act, w_small)


def _decay_kernel(small_t_ref, w2t_ref, fbt_ref, gbt_ref, p_ref, bcum_ref):
    s_len = small_t_ref.shape[1]
    blk = LANES
    n_heads = FOX_HEADS
    n_blk = s_len // blk
    r = lax.broadcasted_iota(jnp.int32, (blk, 2 * blk), 0)
    cc = lax.broadcasted_iota(jnp.int32, (blk, 2 * blk), 1)
    upper_total = jnp.where((r <= cc) | (cc >= blk), 1.0, 0.0).astype(BF16)
    rr = lax.broadcasted_iota(jnp.int32, (4 * n_heads, blk), 0)
    ll = lax.broadcasted_iota(jnp.int32, (4 * n_heads, blk), 1)
    place = jnp.where(rr == ll, 1.0, 0.0).astype(BF16)
    one_row = lax.broadcasted_iota(jnp.int32, (n_heads, blk), 0) == FOX_ONE_LANE - 3 * n_heads

    sums = []
    for n in range(n_blk):
        cols = slice(n * blk, (n + 1) * blk)
        log2_f = _log2_sigmoid(small_t_ref[0:n_heads, cols] + fbt_ref[...])
        sums.append(sum(_dot(p, upper_total) for p in _split3(log2_f)))
    carry = jnp.zeros((n_heads, blk), F32)
    for n in range(n_blk):
        within = sums[n][:, :blk] + carry
        carry = carry + sums[n][:, blk:]
        c1, c2, c3 = _split3(within)
        ones = jnp.where(one_row, 1.0, 0.0).astype(BF16)
        stacked = jnp.concatenate([c1, c2, c3, ones], axis=0)
        p_ref[n * blk:(n + 1) * blk, :] = _dot_tn(stacked, place).astype(BF16)

    r = lax.broadcasted_iota(jnp.int32, (GLA_SUPER, GLA_SUPER), 0)
    cc = lax.broadcasted_iota(jnp.int32, (GLA_SUPER, GLA_SUPER), 1)
    upper = jnp.where((r <= cc) & ((r // GLA_CHUNK) == (cc // GLA_CHUNK)), 1.0, 0.0).astype(BF16)
    gate_bias = _lane_repeat(gbt_ref[...], GLA_SUPER // LANES)
    n_sup = s_len // GLA_SUPER

    def gate(sb):
        cols = slice(sb * GLA_SUPER, (sb + 1) * GLA_SUPER)
        return _dot(w2t_ref[...], small_t_ref[:, cols].astype(BF16)) + gate_bias

    def split(gate_logit):
        log2_a = _log2_sigmoid(gate_logit) * (1.0 / GLA_TAU)
        hi = log2_a.astype(BF16)
        return hi, (log2_a - hi.astype(F32)).astype(BF16)

    def cumsum(sb, parts):
        rows = slice(sb * GLA_SUPER, (sb + 1) * GLA_SUPER)
        bcum_ref[rows, :] = (_dot(parts[0], upper) + _dot(parts[1], upper)).T

    logits, parts = {}, {}
    for step in range(n_sup + 2):
        if step < n_sup:
            logits[step] = gate(step)
        if 1 <= step <= n_sup:
            parts[step - 1] = split(logits.pop(step - 1))
        if step >= 2:
            cumsum(step - 2, parts.pop(step - 2))


def _decay(small_t, w2t, fbias_t, gbias_t, *, batch, seq, layer):
    t = small_t.shape[1]
    gk = w2t.shape[1]
    return pl.pallas_call(
        _decay_kernel,
        grid=(batch,),
        in_specs=[pl.BlockSpec((SMALL_WIDTH, seq), lambda b: (0, b)),
                  _layer((gk, SMALL_WIDTH), layer), _layer((FOX_HEADS, LANES), layer),
                  _layer((gk, LANES), layer)],
        out_specs=[pl.BlockSpec((seq, LANES), lambda b: (b, 0)),
                   pl.BlockSpec((seq, gk), lambda b: (b, 0))],
        out_shape=[jax.ShapeDtypeStruct((t, LANES), BF16), jax.ShapeDtypeStruct((t, gk), F32)],
        compiler_params=_params(1),
        name="decay",
    )(small_t, w2t, fbias_t, gbias_t)


def _fox_kernel(q_ref, k_ref, v_ref, p_ref, o_ref,
                qa_ref, qb_ref, ka_ref, kb_ref, va_ref, vb_ref, m_ref, acc_ref, *, tq):
    s_len = q_ref.shape[0]
    pair = pl.program_id(1)
    n_heads = FOX_HEADS
    one_lane = FOX_ONE_LANE
    lane = lax.broadcasted_iota(jnp.int32, (1, LANES), 1)

    row = lax.broadcasted_iota(jnp.int32, (LANES, 2 * LANES), 0)
    col = lax.broadcasted_iota(jnp.int32, (LANES, 2 * LANES), 1)
    k_side = col >= LANES
    dst = col % LANES
    aug = dst % HEAD_DIM
    head = 2 * pair + jnp.where(dst < HEAD_DIM, 1, 0)
    part = aug - jnp.where(k_side, 3, 0)
    from_c = (part >= 0) & (part < 3) & (row == head + n_heads * part)
    ones_pos = aug - jnp.where(k_side, 0, 3)
    from_one = (row == one_lane) & (ones_pos >= 0) & (ones_pos < 3)
    sel = jnp.where(from_c, jnp.where(k_side, -1.0, 1.0), jnp.where(from_one, 1.0, 0.0))
    spare = _dot(p_ref[...], sel.astype(BF16))
    q_spare = spare[:, :LANES].astype(BF16)
    k_spare = spare[:, LANES:].astype(BF16)

    low = lax.broadcasted_iota(jnp.int32, (s_len, LANES), 1) < HEAD_DIM
    q, k, v = q_ref[...], k_ref[...], v_ref[...]
    qa_ref[...] = jnp.where(low, q, q_spare)
    qb_ref[...] = jnp.where(low, q_spare, q)
    ka_ref[...] = jnp.where(low, k, k_spare)
    kb_ref[...] = jnp.where(low, k_spare, k)
    va_ref[...] = jnp.where(low, v, jnp.where(lane == HEAD_DIM, 1.0, 0.0).astype(BF16))
    vb_ref[...] = jnp.where(low, jnp.where(lane == 0, 1.0, 0.0).astype(BF16), v)

    heads = ((qa_ref, ka_ref, va_ref), (qb_ref, kb_ref, vb_ref))

    def scores(h, q0, nq, k_rows, bias):
        q_ref_h, k_ref_h, _ = heads[h]
        s = _dot_nt(q_ref_h[q0:q0 + nq, :], k_ref_h[k_rows, :])
        return s if bias is None else s + bias

    def start(h, q0, nq, k_rows, bias):
        s = scores(h, q0, nq, k_rows, bias)
        m = jnp.max(s, axis=-1, keepdims=True) + jnp.zeros((nq, LANES), F32)
        p = jnp.exp2(s - _lane_repeat(m, s.shape[1] // LANES))
        acc_ref[h, q0:q0 + nq, :] = _dot(p.astype(BF16), heads[h][2][k_rows, :])
        m_ref[h, q0:q0 + nq, :] = m

    def update(h, q0, nq, k_rows):
        s = scores(h, q0, nq, k_rows, None)
        m_prev = m_ref[h, q0:q0 + nq, :]
        m_new = jnp.maximum(m_prev, jnp.max(s, axis=-1, keepdims=True))
        alpha = jnp.exp2(m_prev - m_new)
        p = jnp.exp2(s - _lane_repeat(m_new, s.shape[1] // LANES))
        acc_ref[h, q0:q0 + nq, :] = (alpha * acc_ref[h, q0:q0 + nq, :]
                                     + _dot(p.astype(BF16), heads[h][2][k_rows, :]))
        m_ref[h, q0:q0 + nq, :] = m_new

    half = tq // 2
    ri = lax.broadcasted_iota(jnp.int32, (half, tq), 0)
    ci = lax.broadcasted_iota(jnp.int32, (half, tq), 1)
    upper_bias = jnp.where(lax.broadcasted_iota(jnp.int32, (half, half), 1)
                           <= lax.broadcasted_iota(jnp.int32, (half, half), 0), 0.0, NEG_INF)
    lower_bias = jnp.where(ci <= ri + half, 0.0, NEG_INF)
    n_tiles = s_len // tq
    for qi in range(n_tiles):
        q0 = qi * tq
        for h in range(2):
            start(h, q0, half, slice(q0, q0 + half), upper_bias)
            start(h, q0 + half, half, slice(q0, q0 + tq), lower_bias)

    for qi in range(1, n_tiles):
        def kv_step(kj, _, q0=qi * tq):
            k_rows = pl.ds(pl.multiple_of(kj * tq, tq), tq)
            for h in range(2):
                update(h, q0, tq, k_rows)
            return 0

        lax.fori_loop(0, qi, kv_step, 0)

    acc_a, acc_b = acc_ref[0], acc_ref[1]
    l_a = jnp.sum(jnp.where(lane == HEAD_DIM, acc_a, 0.0), axis=-1, keepdims=True)
    l_b = jnp.sum(jnp.where(lane == 0, acc_b, 0.0), axis=-1, keepdims=True)
    o_ref[...] = jnp.where(lane < HEAD_DIM, acc_a / l_a, acc_b / l_b).astype(BF16)


def _fox(raw, c_split, *, batch, seq):
    t = raw.shape[0]
    tq = min(FOX_Q_TILE, seq)
    n_pairs = FOX_HEADS // 2
    blk = lambda off: pl.BlockSpec((seq, LANES), lambda b, j: (b, off // LANES + j))
    return pl.pallas_call(
        functools.partial(_fox_kernel, tq=tq),
        grid=(batch, n_pairs),
        in_specs=[blk(RAW_FQ), blk(RAW_FK), blk(RAW_FV),
                  pl.BlockSpec((seq, LANES), lambda b, j: (b, 0))],
        out_specs=pl.BlockSpec((seq, LANES), lambda b, j: (b, j)),
        out_shape=jax.ShapeDtypeStruct((t, FOX_HEADS * HEAD_DIM), BF16),
        scratch_shapes=[pltpu.VMEM((seq, LANES), BF16)] * 6
                       + [pltpu.VMEM((2, seq, LANES), F32)] * 2,
        compiler_params=_params(2),
        name="fox",
    )(raw, raw, raw, c_split)


def _swa_kernel(sink_ref, q_ref, k_ref, v_ref, o_ref):
    s_len = q_ref.shape[0]
    w = WINDOW
    group = SWA_Q_HEADS // SWA_KV_HEADS
    scale = HEAD_DIM ** -0.5 * LOG2E
    low = lax.broadcasted_iota(jnp.int32, (w, LANES), 1) < HEAD_DIM
    i = lax.broadcasted_iota(jnp.int32, (w, 2 * w), 0)
    j = lax.broadcasted_iota(jnp.int32, (w, 2 * w), 1)
    band = jnp.where((j > i) & (j <= i + w), 0.0, NEG_INF)
    band = jnp.concatenate([band] * group, axis=0)
    it = lax.broadcasted_iota(jnp.int32, (w, w), 0)
    jt = lax.broadcasted_iota(jnp.int32, (w, w), 1)
    tri = jnp.concatenate([jnp.where(jt <= it, 0.0, NEG_INF)] * group, axis=0)
    hrow = lax.broadcasted_iota(jnp.int32, (group * w, LANES), 0) // w

    def block(n, sinks, first):
        q_rows = pl.ds(pl.multiple_of(n * w, w), w)
        k_rows = q_rows if first else pl.ds(pl.multiple_of((n - 1) * w, w), 2 * w)
        kk = k_ref[k_rows, :]
        vv = v_ref[k_rows, :]
        q_blocks = [q_ref[q_rows, g * LANES:(g + 1) * LANES] for g in range(group)]
        outs = []
        for kvh in range(SWA_KV_HEADS):
            keep = low if kvh == 0 else jnp.logical_not(low)
            q = jnp.concatenate([jnp.where(keep, qb, jnp.zeros_like(qb)) for qb in q_blocks],
                                axis=0)
            s = _dot_nt(q, kk) * scale + (tri if first else band)
            m = jnp.maximum(jnp.max(s, axis=-1, keepdims=True), sinks[kvh])
            p = jnp.exp2(s - (m if first else _lane_repeat(m, 2 * w // LANES)))
            denom = jnp.sum(p, axis=-1, keepdims=True) + jnp.exp2(sinks[kvh] - m)
            outs.append(_dot(p.astype(BF16), vv) / denom)
        for g in range(group):
            rows = slice(g * w, (g + 1) * w)
            o_ref[q_rows, g * LANES:(g + 1) * LANES] = jnp.where(
                low, outs[0][rows, :], outs[1][rows, :]).astype(BF16)

    sinks = []
    for kvh in range(SWA_KV_HEADS):
        sink = jnp.zeros((group * w, LANES), F32)
        for g in range(group):
            sink = jnp.where(hrow == g, sink_ref[kvh * group + g] * LOG2E, sink)
        sinks.append(sink)
    block(0, sinks, True)

    def body(n, _):
        block(n, sinks, False)
        return 0

    lax.fori_loop(1, s_len // w, body, 0, unroll=SWA_UNROLL)


def _swa(raw, sinks, *, batch, seq):
    t = raw.shape[0]
    qw = SWA_Q_HEADS * HEAD_DIM
    kw = SWA_KV_HEADS * HEAD_DIM
    return pl.pallas_call(
        _swa_kernel,
        grid=(batch,),
        in_specs=[pl.BlockSpec(memory_space=pltpu.SMEM),
                  pl.BlockSpec((seq, qw), lambda b: (b, RAW_SQ // qw)),
                  pl.BlockSpec((seq, kw), lambda b: (b, RAW_SK // kw)),
                  pl.BlockSpec((seq, kw), lambda b: (b, RAW_SV // kw))],
        out_specs=pl.BlockSpec((seq, qw), lambda b: (b, 0)),
        out_shape=jax.ShapeDtypeStruct((t, qw), BF16),
        compiler_params=_params(1),
        name="swa",
    )(sinks, raw, raw, raw)


def _gla_kernel(q_ref, k_ref, v_ref, b_ref, gr_ref, nw_ref, o_ref, st_ref, *, n_heads, unroll):
    s_len = q_ref.shape[0]
    ck, sup = GLA_CHUNK, GLA_SUPER
    dk, dv = GLA_HEAD_K, GLA_HEAD_V
    per = sup // ck
    r = lax.broadcasted_iota(jnp.int32, (sup, sup), 0)
    c = lax.broadcasted_iota(jnp.int32, (sup, sup), 1)
    intra = (r >= c) & ((r // ck) == (c // ck))
    st_ref[...] = jnp.zeros(st_ref.shape, F32)

    def decayed(rows, h):
        kcols = slice(h * dk, (h + 1) * dk)
        b = b_ref[rows, kcols]
        b3 = b.reshape(per, ck, dk)
        b_last = b3[:, ck - 1:ck, :]
        qf = q_ref[rows, kcols].astype(F32)
        kf = k_ref[rows, kcols].astype(F32)
        q_t = (qf * jnp.exp2(b)).astype(BF16)
        k_t = (kf * jnp.exp2(-b)).astype(BF16)
        k_s = (kf.reshape(per, ck, dk) * jnp.exp2(b_last - b3)).astype(BF16)
        return q_t, k_t, k_s, jnp.exp2(b_last), v_ref[rows, h * dv:(h + 1) * dv]

    def products(q_t, k_t, k_s, decay, v):
        qk = _dot_nt(q_t, k_t)
        kv = [_dot_tn(v[c * ck:(c + 1) * ck, :], k_s[c]) for c in range(per)]
        return qk, kv

    def intra_chunk(qk, v):
        return _dot(jnp.where(intra, qk, 0.0).astype(BF16), v)

    def carry_state(h, q_t, decay, kv, o_intra):
        st = st_ref[h]
        outs = []
        for c in range(per):
            chunk = slice(c * ck, (c + 1) * ck)
            outs.append(o_intra[chunk, :] + _dot_nt(q_t[chunk, :], st.astype(BF16)))
            st = st * decay[c] + kv[c]
        st_ref[h] = st
        return jnp.concatenate(outs, axis=0)

    def finish(rows, h, o):
        vcols = slice(h * dv, (h + 1) * dv)
        o = o * lax.rsqrt(jnp.mean(o * o, axis=-1, keepdims=True) + EPS) * nw_ref[...]
        o_ref[rows, vcols] = (o * gr_ref[rows, vcols].astype(F32)).astype(BF16)

    def step(i, _):
        work = []
        for u in range(unroll):
            start = pl.multiple_of((i * unroll + u) * sup, sup)
            work += [(pl.ds(start, sup), h) for h in range(n_heads)]
        pre = [decayed(rows, h) for rows, h in work]
        prod = [products(*p) for p in pre]
        o_intra = [intra_chunk(qk, p[4]) for (qk, _), p in zip(prod, pre)]
        outs = [carry_state(h, p[0], p[3], kv, oi)
                for (_, h), p, (_, kv), oi in zip(work, pre, prod, o_intra)]
        for (rows, h), o in zip(work, outs):
            finish(rows, h, o)
        return 0

    lax.fori_loop(0, s_len // (sup * unroll), step, 0)


def _gla(raw, bcum, act, gla_norm, *, batch, seq, heads_per_step=2, unroll=4):
    t = raw.shape[0]
    dk, dv = GLA_HEAD_K * heads_per_step, GLA_HEAD_V * heads_per_step
    assert RAW_GQ % dk == 0 and RAW_GK % dk == 0 and RAW_GV % dv == 0
    return pl.pallas_call(
        functools.partial(_gla_kernel, n_heads=heads_per_step, unroll=unroll),
        grid=(batch, GLA_HEADS // heads_per_step),
        in_specs=[pl.BlockSpec((seq, dk), lambda b, h: (b, RAW_GQ // dk + h)),
                  pl.BlockSpec((seq, dk), lambda b, h: (b, RAW_GK // dk + h)),
                  pl.BlockSpec((seq, dv), lambda b, h: (b, RAW_GV // dv + h)),
                  pl.BlockSpec((seq, dk), lambda b, h: (b, h)),
                  pl.BlockSpec((seq, dv), lambda b, h: (b, h)),
                  pl.BlockSpec((1, GLA_HEAD_V), lambda b, h: (0, 0))],
        out_specs=pl.BlockSpec((seq, dv), lambda b, h: (b, h)),
        out_shape=jax.ShapeDtypeStruct((t, GLA_HEADS * GLA_HEAD_V), BF16),
        scratch_shapes=[pltpu.VMEM((heads_per_step, GLA_HEAD_V, GLA_HEAD_K), F32)],
        compiler_params=_params(2),
        name="gla",
    )(raw, raw, raw, bcum, act, gla_norm)


def _merge_kernel(x_ref, of_ref, os_ref, og_ref, g0_ref, g1_ref, g2_ref,
                  wf_ref, ws_ref, wg_ref, wo_ref, o_ref):
    merged = (g0_ref[...].astype(F32) * _dot(of_ref[...], wf_ref[...])
              + g1_ref[...].astype(F32) * _dot(os_ref[...], ws_ref[...])
              + g2_ref[...].astype(F32) * _dot(og_ref[...], wg_ref[...]))
    o_ref[...] = x_ref[...] + _dot(merged.astype(BF16), wo_ref[...])


def _merge(x, o_fox, o_swa, o_gla, act, wf, ws, wg, wo, *, layer, tm=1024):
    t, d = x.shape
    row = lambda w, j=0: pl.BlockSpec((tm, w), lambda i: (i, j))
    return pl.pallas_call(
        _merge_kernel,
        grid=(t // tm,),
        in_specs=[row(d), row(o_fox.shape[1]), row(o_swa.shape[1]), row(o_gla.shape[1]),
                  row(d, 1), row(d, 2), row(d, 3),
                  _layer(wf.shape[1:], layer), _layer(ws.shape[1:], layer),
                  _layer(wg.shape[1:], layer), _layer(wo.shape[1:], layer)],
        out_specs=row(d),
        out_shape=jax.ShapeDtypeStruct((t, d), F32),
        compiler_params=_params(1),
        name="merge",
    )(x, o_fox, o_swa, o_gla, act, act, act, wf, ws, wg, wo)


def _w_in_pieces():
    sizes = (512, 512, 512, FOX_HEADS, 512, 128, 128, 512, 512, 1024, GLA_RANK, 1024, 3072)
    offs = [0]
    for s in sizes:
        offs.append(offs[-1] + s)
    fq, fk, fv, ff, sq, sk, sv, gq, gk, gv, glr, gr, gates = offs[:-1]
    h = HEAD_DIM
    raw = ([(fq, 512), (fk, 512), (fv, 512)] + [(sq + h * i, h) for i in SWA_HEAD_ORDER]
           + [(gq, 512), (gk, 512), (gv, 1024), (sk, 2 * h), (sv, 2 * h)])
    moves, dst = [], 0
    for src, width in raw:
        moves.append(("raw", dst, src, width))
        dst += width
    assert dst == RAW_WIDTH
    moves.append(("act", 0, gr, ACT_WIDTH))
    moves += [("small", 0, ff, FOX_HEADS), ("small", FOX_HEADS, glr, GLA_RANK)]
    return moves


def _w_in_kernel(w_ref, raw_ref, act_ref, small_ref):
    outs = {"raw": raw_ref, "act": act_ref, "small": small_ref}
    small_ref[...] = jnp.zeros(small_ref.shape, BF16)
    for name, dst, src, width in _w_in_pieces():
        outs[name][dst:dst + width, :] = w_ref[src:src + width, :].astype(BF16)


def _prep_w_in(w_in, *, cols=256):
    depth, d, d_in = w_in.shape
    w_t = jnp.swapaxes(w_in, 1, 2)
    out = lambda n: pl.BlockSpec((None, n, cols), lambda l, i: (l, 0, i))
    return pl.pallas_call(
        _w_in_kernel,
        grid=(depth, d // cols),
        in_specs=[pl.BlockSpec((None, d_in, cols), lambda l, i: (l, 0, i))],
        out_specs=[out(RAW_WIDTH), out(ACT_WIDTH), out(SMALL_WIDTH)],
        out_shape=[jax.ShapeDtypeStruct((depth, RAW_WIDTH, d), BF16),
                   jax.ShapeDtypeStruct((depth, ACT_WIDTH, d), BF16),
                   jax.ShapeDtypeStruct((depth, SMALL_WIDTH, d), BF16)],
        compiler_params=_params(2),
        name="w_in_layout",
    )(w_t)


def kernel(x, ffn1_norm, ffn1_w_gu, ffn1_w_down, mix_norm, w_in, fox_forget_bias, swa_sinks,
           gla_gate_w2, gla_gate_bias, gla_norm, w_branch_fox, w_branch_swa, w_branch_gla, w_out,
           ffn2_norm, ffn2_w_gu, ffn2_w_down, final_norm):
    batch, seq, d = x.shape
    depth = w_in.shape[0]
    xt = x.reshape(batch * seq, d)

    w_raw, w_act, w_small = _prep_w_in(w_in)
    gk = gla_gate_w2.shape[-1]
    w2t = jnp.zeros((depth, gk, SMALL_WIDTH), BF16)
    w2t = w2t.at[:, :, FOX_HEADS:FOX_HEADS + GLA_RANK].set(
        jnp.swapaxes(gla_gate_w2, 1, 2).astype(BF16))
    fbias_t = jnp.broadcast_to(fox_forget_bias[:, :, None], (depth, FOX_HEADS, LANES))
    gbias_t = jnp.broadcast_to(gla_gate_bias[:, :, None], (depth, gk, LANES))
    f1gu, f1d = ffn1_w_gu.astype(BF16), ffn1_w_down.astype(BF16)
    f2gu, f2d = ffn2_w_gu.astype(BF16), ffn2_w_down.astype(BF16)
    wbf = w_branch_fox.astype(BF16)
    swa_rows = w_branch_swa.reshape(depth, SWA_Q_HEADS, HEAD_DIM, d)[:, jnp.array(SWA_HEAD_ORDER)]
    wbs = swa_rows.reshape(depth, SWA_Q_HEADS * HEAD_DIM, d).astype(BF16)
    wbg, wo = w_branch_gla.astype(BF16), w_out.astype(BF16)
    fw = final_norm.reshape(1, d)

    for l in range(depth):
        xt = _ffn(xt, ffn1_norm[l].reshape(1, d), f1gu, f1d, fw, layer=l, final_norm=False)
        raw, act, small_t = _inproj(xt, mix_norm[l].reshape(1, d), w_raw, w_act, w_small, layer=l)
        c_split, bcum = _decay(small_t, w2t, fbias_t, gbias_t, batch=batch, seq=seq, layer=l)
        o_fox = _fox(raw, c_split, batch=batch, seq=seq)
        o_swa = _swa(raw, swa_sinks[l], batch=batch, seq=seq)
        o_gla = _gla(raw, bcum, act, gla_norm[l].reshape(1, -1), batch=batch, seq=seq)
        xt = _merge(xt, o_fox, o_swa, o_gla, act, wbf, wbs, wbg, wo, layer=l)
        xt = _ffn(xt, ffn2_norm[l].reshape(1, d), f2gu, f2d, fw, layer=l,
                  final_norm=(l == depth - 1))
    return xt.reshape(batch, seq, d)
```
